```python
import math
import jax, jax.numpy as jnp
from jax import lax
import numpy as np

D_MODEL = 2048
BATCH = 2
SEQ = 4096
DEPTH = 4
DEC_BATCH = 128
DEC_SEQ = 8
PAST_LEN = 8192
PAGE_SIZE = 128

N_MIXERS = 2
N_A_LAYERS = (DEPTH + 1) // 2
N_B_LAYERS = DEPTH // 2

HG_HEAD_V = 128
HG_HEADS = D_MODEL // HG_HEAD_V
HG_EXPAND = 128
HG_F = HG_HEADS * HG_EXPAND
HG_V = HG_HEADS * HG_HEAD_V
HG_CHUNK = 64
FORGET_FLOOR = 1e-30

MLA_HEADS = 16
Q_LORA = 512
KV_LORA = 512
QK_NOPE = 128
QK_ROPE = 64
V_HEAD = 128
QK_DIM = QK_NOPE + QK_ROPE
ATTN_SCALE = QK_DIM ** -0.5
ROPE_THETA = 10000.0
Q_BLOCK = 128
MASK_VALUE = -1e30

N_EXPERTS = 32
TOP_K = 4
D_FF = D_MODEL
SWIGLU_LIMIT = 7.0
SWIGLU_ALPHA = 1.702
ROW_BLOCK = 128

DEEPNORM_ALPHA = (2 * DEPTH) ** 0.25
DEEPNORM_BETA = (8 * DEPTH) ** -0.25
NORM_EPS = 1e-5
ADA_INIT = 0.2

kernel_name = "hgrn2_mla_moe_deepnorm_adaln_step"

F32 = jnp.float32


def layer_norm(x, g, b):
    xf = x.astype(F32)
    mu = jnp.mean(xf, -1, keepdims=True)
    var = jnp.mean(jnp.square(xf - mu), -1, keepdims=True)
    return ((xf - mu) * lax.rsqrt(var + NORM_EPS) * g + b).astype(x.dtype)


def rms_norm(x, g):
    xf = x.astype(F32)
    return (xf * lax.rsqrt(jnp.mean(xf * xf, -1, keepdims=True) + NORM_EPS) * g).astype(x.dtype)


def rope(x, pos):
    half = x.shape[-1] // 2
    inv = ROPE_THETA ** (-jnp.arange(half, dtype=F32) / half)
    ang = pos.astype(F32)[:, None] * inv[None, :]
    cos = jnp.cos(ang)[None, :, None, :]
    sin = jnp.sin(ang)[None, :, None, :]
    xf = x.astype(F32)
    x1, x2 = xf[..., :half], xf[..., half:]
    return jnp.concatenate([x1 * cos - x2 * sin, x1 * sin + x2 * cos], -1).astype(x.dtype)


def adaln_mods(c, w_ada, b_ada):
    m = jax.nn.silu(c) @ w_ada + b_ada
    return jnp.split(m[:, None, :], 6, axis=-1)


def modulate(x, shift, scale):
    return x * (1 + scale) + shift


def deepnorm_residual(x, y, gate, g, b):
    return layer_norm(DEEPNORM_ALPHA * x + (1 + gate) * y, g, b)


def chunked_gated_recurrence(q, k, v, log_f, s0):
    B, T, H, N = q.shape
    V = v.shape[-1]
    C = math.gcd(T, HG_CHUNK)
    nc = T // C

    def chunks(a):
        return a.reshape(B, nc, C, H, a.shape[-1]).transpose(1, 0, 3, 2, 4)

    causal = jnp.tril(jnp.ones((C, C), dtype=bool))[:, :, None]
    causal_f = causal.astype(F32)

    def step(s, xs):
        qc, kc, vc, gc = xs
        b = jnp.cumsum(gc, axis=2)
        diff = b[:, :, :, None, :] - b[:, :, None, :, :]
        rel = jnp.exp(jnp.where(causal, diff, 0.0)) * causal_f
        attn = jnp.einsum('bhtn,bhsn,bhtsn->bhts', qc, kc, rel)
        o = (jnp.einsum('bhtn,bhnv->bhtv', qc * jnp.exp(b), s)
             + jnp.einsum('bhts,bhsv->bhtv', attn, vc))
        b_end = b[:, :, -1:, :]
        s = (jnp.exp(b_end)[:, :, 0, :, None] * s
             + jnp.einsum('bhsn,bhsv->bhnv', kc * jnp.exp(b_end - b), vc))
        return s, o

    s_final, o = lax.scan(step, s0, (chunks(q), chunks(k), chunks(v), chunks(log_f)))
    return o.transpose(1, 0, 3, 2, 4).reshape(B, T, H, V), s_final


def hgrn2_mixer(h, s0, lb, w_in, norm_g, w_out):
    B, T, _ = h.shape
    proj = h @ w_in
    q, f, i, g = jnp.split(proj, [HG_F, 2 * HG_F, 2 * HG_F + HG_V], axis=-1)
    lb = lb.astype(F32)
    f = f.astype(F32)
    forget = lb + (1.0 - lb) * jax.nn.sigmoid(f)
    log_forget = jnp.log(jnp.maximum(forget, FORGET_FLOOR))
    key_in = (1.0 - lb) * jax.nn.sigmoid(-f)

    def heads(a, d):
        return a.reshape(B, T, HG_HEADS, d)

    o, s_new = chunked_gated_recurrence(
        heads(jax.nn.silu(q.astype(F32)), HG_EXPAND),
        heads(key_in, HG_EXPAND),
        heads(i.astype(F32), HG_HEAD_V),
        heads(log_forget, HG_EXPAND),
        s0.astype(F32))
    o = rms_norm(o, norm_g) * heads(jax.nn.sigmoid(g.astype(F32)), HG_HEAD_V)
    y = o.reshape(B, T, HG_V).astype(h.dtype) @ w_out
    return y, s_new.astype(s0.dtype)


def mla_project(h, pos, w_in, q_norm_g, kv_norm_g, w_q_up):
    B, T, _ = h.shape
    a = h @ w_in
    cq = rms_norm(a[..., :Q_LORA], q_norm_g)
    ckv = rms_norm(a[..., Q_LORA:Q_LORA + KV_LORA], kv_norm_g)
    kr = rope(a[..., None, Q_LORA + KV_LORA:], pos)[:, :, 0, :]
    q = (cq @ w_q_up).reshape(B, T, MLA_HEADS, QK_DIM)
    return q[..., :QK_NOPE], rope(q[..., QK_NOPE:], pos), ckv, kr


def split_kv_up(w_kv_up):
    w = w_kv_up.reshape(KV_LORA, MLA_HEADS, QK_NOPE + V_HEAD)
    return w[..., :QK_NOPE], w[..., QK_NOPE:]


def mla_prompt(h, w_in, q_norm_g, kv_norm_g, w_q_up, w_kv_up, w_out):
    B, T, _ = h.shape
    q_nope, q_rope, ckv, kr = mla_project(h, jnp.arange(T), w_in, q_norm_g, kv_norm_g, w_q_up)
    w_uk, w_uv = split_kv_up(w_kv_up)
    k_nope = jnp.einsum('btc,chn->bthn', ckv, w_uk)
    v = jnp.einsum('btc,chv->bthv', ckv, w_uv)
    q = jnp.concatenate([q_nope, q_rope], -1)
    k = jnp.concatenate([k_nope, jnp.broadcast_to(kr[:, :, None, :], (B, T, MLA_HEADS, QK_ROPE))], -1)
    qb_len = math.gcd(T, Q_BLOCK)
    nb = T // qb_len
    q_blocks = q.reshape(B, nb, qb_len, MLA_HEADS, QK_DIM).swapaxes(0, 1)
    kpos = jnp.arange(T)

    def block(args):
        qblk, start = args
        s = jnp.einsum('bqhd,bkhd->bhqk', qblk, k).astype(F32) * ATTN_SCALE
        qpos = start + jnp.arange(qb_len)
        s = jnp.where(kpos[None, :] <= qpos[:, None], s, MASK_VALUE)
        p = jax.nn.softmax(s, axis=-1).astype(v.dtype)
        return jnp.einsum('bhqk,bkhv->bqhv', p, v)

    o = lax.map(block, (q_blocks, jnp.arange(nb) * qb_len))
    o = o.swapaxes(0, 1).reshape(B, T, MLA_HEADS * V_HEAD)
    return o @ w_out, ckv, kr


def mla_sample(h, cache_lat, cache_kr, page_table, w_in, q_norm_g, kv_norm_g, w_q_up, w_kv_up, w_out):
    B, T, _ = h.shape
    past = page_table.shape[1] * PAGE_SIZE
    q_nope, q_rope, ckv, kr = mla_project(h, past + jnp.arange(T), w_in, q_norm_g, kv_norm_g, w_q_up)
    w_uk, w_uv = split_kv_up(w_kv_up)
    q_lat = jnp.einsum('bqhn,chn->bqhc', q_nope, w_uk)
    lat_past = cache_lat[page_table].reshape(B, past, KV_LORA)
    kr_past = cache_kr[page_table].reshape(B, past, QK_ROPE)
    s_past = (jnp.einsum('bqhc,bkc->bhqk', q_lat, lat_past)
              + jnp.einsum('bqhr,bkr->bhqk', q_rope, kr_past)).astype(F32)
    s_new = (jnp.einsum('bqhc,bkc->bhqk', q_lat, ckv)
             + jnp.einsum('bqhr,bkr->bhqk', q_rope, kr)).astype(F32)
    s_new = jnp.where(jnp.tril(jnp.ones((T, T), dtype=bool)), s_new * ATTN_SCALE, MASK_VALUE)
    s = jnp.concatenate([s_past * ATTN_SCALE, s_new], -1)
    p = jax.nn.softmax(s, axis=-1).astype(ckv.dtype)
    o_lat = (jnp.einsum('bhqk,bkc->bqhc', p[..., :past], lat_past)
             + jnp.einsum('bhqk,bkc->bqhc', p[..., past:], ckv))
    o = jnp.einsum('bqhc,chv->bqhv', o_lat, w_uv).reshape(B, T, MLA_HEADS * V_HEAD)
    return o @ w_out, ckv, kr


def moe_ffn(h, w_router, b_router, w_gu, b_gu, w_down, b_down):
    B, T, D = h.shape
    xt = h.reshape(B * T, D)
    n = xt.shape[0]
    n_assign = n * TOP_K
    logits = (xt @ w_router + b_router).astype(F32)
    top_val, top_idx = lax.top_k(logits, TOP_K)
    gates = jax.nn.softmax(top_val, axis=-1)
    flat_e = top_idx.reshape(n_assign)
    flat_tok = jnp.repeat(jnp.arange(n, dtype=jnp.int32), TOP_K)
    flat_w = gates.reshape(n_assign)
    onehot = jax.nn.one_hot(flat_e, N_EXPERTS, dtype=jnp.int32)
    rank = jnp.sum((jnp.cumsum(onehot, 0) - 1) * onehot, axis=1)
    counts = jnp.sum(onehot, 0)
    padded = (counts + ROW_BLOCK - 1) // ROW_BLOCK * ROW_BLOCK
    pad_end = jnp.cumsum(padded)
    dest = (pad_end - padded)[flat_e] + rank
    n_blocks = -(-n_assign // ROW_BLOCK) + N_EXPERTS
    n_rows = n_blocks * ROW_BLOCK
    row_tok = jnp.zeros((n_rows,), jnp.int32).at[dest].set(flat_tok)
    row_w = jnp.zeros((n_rows,), F32).at[dest].set(flat_w)
    block_e = jnp.minimum(
        jnp.searchsorted(pad_end, jnp.arange(n_blocks, dtype=jnp.int32) * ROW_BLOCK, side='right'),
        N_EXPERTS - 1)
    xb = xt[row_tok].reshape(n_blocks, ROW_BLOCK, D)

    def expert_block(args):
        xblk, e = args
        gu = xblk @ w_gu[e] + b_gu[e]
        gate = jnp.minimum(gu[:, :D_FF], SWIGLU_LIMIT)
        up = jnp.clip(gu[:, D_FF:], -SWIGLU_LIMIT, SWIGLU_LIMIT)
        act = (up + 1) * (gate * jax.nn.sigmoid(SWIGLU_ALPHA * gate))
        return act @ w_down[e] + b_down[e]

    yb = lax.map(expert_block, (xb, block_e)).reshape(n_rows, D)
    y = jnp.zeros_like(xt).at[row_tok].add(row_w[:, None].astype(xt.dtype) * yb)
    return y.reshape(B, T, D)


def setup_inputs(seed: int = 0) -> dict:
    key = jax.random.key(seed)
    ks = iter(jax.random.split(key, 48))

    def nrm(shape, s):
        return jax.random.normal(next(ks), shape, F32) * s

    D = D_MODEL
    n_pages = PAST_LEN // PAGE_SIZE
    n_used = DEC_BATCH * n_pages
    n_pool = n_used + n_used // 4
    page_table = jax.random.permutation(next(ks), n_pool)[:n_used].reshape(DEC_BATCH, n_pages).astype(jnp.int32)
    beta = DEEPNORM_BETA
    hg_w_in = jnp.concatenate([
        nrm((N_A_LAYERS, D, HG_F), D ** -0.5),
        nrm((N_A_LAYERS, D, HG_F), D ** -0.5),
        nrm((N_A_LAYERS, D, HG_V), D ** -0.5 * beta),
        nrm((N_A_LAYERS, D, HG_V), D ** -0.5),
    ], axis=-1)
    mla_w_kv_up = jnp.concatenate([
        nrm((N_B_LAYERS, KV_LORA, MLA_HEADS, QK_NOPE), KV_LORA ** -0.5),
        nrm((N_B_LAYERS, KV_LORA, MLA_HEADS, V_HEAD), KV_LORA ** -0.5 * beta),
    ], axis=-1).reshape(N_B_LAYERS, KV_LORA, MLA_HEADS * (QK_NOPE + V_HEAD))
    return {
        "x_prompt": nrm((BATCH, SEQ, D), 1.0),
        "x_sample": nrm((DEC_BATCH, DEC_SEQ, D), 1.0),
        "c_prompt": nrm((BATCH, D), 1.0),
        "c_sample": nrm((DEC_BATCH, D), 1.0),
        "cache_mla_latent": nrm((N_B_LAYERS, n_pool, PAGE_SIZE, KV_LORA), 1.0),
        "cache_mla_krope": nrm((N_B_LAYERS, n_pool, PAGE_SIZE, QK_ROPE), 1.0),
        "state_hgrn": nrm((N_A_LAYERS, DEC_BATCH, HG_HEADS, HG_EXPAND, HG_HEAD_V), 0.5),
        "page_table": page_table,
        "ada_w": nrm((DEPTH, D, 6 * D), ADA_INIT * D ** -0.5),
        "ada_b": nrm((DEPTH, 6 * D), 0.01),
        "ln_g": 1.0 + nrm((DEPTH, 2, D), 0.01),
        "ln_b": nrm((DEPTH, 2, D), 0.01),
        "hg_lb_logits": nrm((N_A_LAYERS, HG_F), 0.5),
        "hg_w_in": hg_w_in,
        "hg_norm_g": 1.0 + nrm((N_A_LAYERS, HG_HEAD_V), 0.01),
        "hg_w_out": nrm((N_A_LAYERS, HG_V, D), HG_V ** -0.5 * beta),
        "mla_w_in": nrm((N_B_LAYERS, D, Q_LORA + KV_LORA + QK_ROPE), D ** -0.5),
        "mla_q_norm_g": 1.0 + nrm((N_B_LAYERS, Q_LORA), 0.01),
        "mla_kv_norm_g": 1.0 + nrm((N_B_LAYERS, KV_LORA), 0.01),
        "mla_w_q_up": nrm((N_B_LAYERS, Q_LORA, MLA_HEADS * QK_DIM), Q_LORA ** -0.5),
        "mla_w_kv_up": mla_w_kv_up,
        "mla_w_out": nrm((N_B_LAYERS, MLA_HEADS * V_HEAD, D), (MLA_HEADS * V_HEAD) ** -0.5 * beta),
        "moe_w_router": nrm((DEPTH, D, N_EXPERTS), D ** -0.5),
        "moe_b_router": nrm((DEPTH, N_EXPERTS), 0.01),
        "moe_w_gu": nrm((DEPTH, N_EXPERTS, D, 2 * D_FF), D ** -0.5 * beta),
        "moe_b_gu": nrm((DEPTH, N_EXPERTS, 2 * D_FF), 0.01),
        "moe_w_down": nrm((DEPTH, N_EXPERTS, D_FF, D), D_FF ** -0.5 * beta),
        "moe_b_down": nrm((DEPTH, N_EXPERTS, D), 0.01),
    }


def reference(x_prompt, x_sample, c_prompt, c_sample, cache_mla_latent, cache_mla_krope, state_hgrn,
              page_table, ada_w, ada_b, ln_g, ln_b, hg_lb_logits, hg_w_in, hg_norm_g, hg_w_out,
              mla_w_in, mla_q_norm_g, mla_kv_norm_g, mla_w_q_up, mla_w_kv_up, mla_w_out,
              moe_w_router, moe_b_router, moe_w_gu, moe_b_gu, moe_w_down, moe_b_down):
    lb_p = jax.nn.softmax(hg_lb_logits.astype(F32), axis=0)
    lower_bounds = jnp.cumsum(lb_p, axis=0) - lb_p[0]
    xp, xs = x_prompt, x_sample
    lat_p, kr_p, hg_p, lat_s, kr_s, hg_s = [], [], [], [], [], []
    for l in range(DEPTH):
        j = l // N_MIXERS
        mp = adaln_mods(c_prompt, ada_w[l], ada_b[l])
        ms = adaln_mods(c_sample, ada_w[l], ada_b[l])
        hp = modulate(xp, mp[0], mp[1])
        hs = modulate(xs, ms[0], ms[1])
        if l % N_MIXERS == 0:
            s0 = jnp.zeros((xp.shape[0], HG_HEADS, HG_EXPAND, HG_HEAD_V), xp.dtype)
            yp, sp = hgrn2_mixer(hp, s0, lower_bounds[j], hg_w_in[j], hg_norm_g[j], hg_w_out[j])
            ys, ss = hgrn2_mixer(hs, state_hgrn[j], lower_bounds[j], hg_w_in[j], hg_norm_g[j], hg_w_out[j])
            hg_p.append(sp)
            hg_s.append(ss)
        else:
            yp, lp, kp = mla_prompt(hp, mla_w_in[j], mla_q_norm_g[j], mla_kv_norm_g[j],
                                    mla_w_q_up[j], mla_w_kv_up[j], mla_w_out[j])
            ys, ls, ksm = mla_sample(hs, cache_mla_latent[j], cache_mla_krope[j], page_table,
                                     mla_w_in[j], mla_q_norm_g[j], mla_kv_norm_g[j],
                                     mla_w_q_up[j], mla_w_kv_up[j], mla_w_out[j])
            lat_p.append(lp)
            kr_p.append(kp)
            lat_s.append(ls)
            kr_s.append(ksm)
        xp = deepnorm_residual(xp, yp, mp[2], ln_g[l, 0], ln_b[l, 0])
        xs = deepnorm_residual(xs, ys, ms[2], ln_g[l, 0], ln_b[l, 0])
        moe_args = (moe_w_router[l], moe_b_router[l], moe_w_gu[l], moe_b_gu[l], moe_w_down[l], moe_b_down[l])
        xp = deepnorm_residual(xp, moe_ffn(modulate(xp, mp[3], mp[4]), *moe_args), mp[5], ln_g[l, 1], ln_b[l, 1])
        xs = deepnorm_residual(xs, moe_ffn(modulate(xs, ms[3], ms[4]), *moe_args), ms[5], ln_g[l, 1], ln_b[l, 1])
    return (xp, xs, jnp.stack(lat_p), jnp.stack(kr_p), jnp.stack(hg_p),
            jnp.stack(lat_s), jnp.stack(kr_s), jnp.stack(hg_s))
```

```python
import functools
import math

import jax
import jax.numpy as jnp
from jax import lax
from jax.experimental import pallas as pl
from jax.experimental.pallas import tpu as pltpu

F32 = jnp.float32
BF16 = jnp.bfloat16
I32 = jnp.int32

LANES = 128
VMEM_LIMIT_BYTES = 56 * 1024 * 1024

HEAD = 128
QK_ROPE = 64
Q_LORA = 512
KV_LORA = 512
TOP_K = 4
SWIGLU_LIMIT = 7.0
SWIGLU_ALPHA = 1.702
NORM_EPS = 1e-5
FORGET_FLOOR = 1e-30
ROPE_THETA = 10000.0
MASK_VALUE = -1e30
HIGHEST = lax.Precision.HIGHEST

LN_TM = 256
MM_TM = 512
HG_TB = 128
HG_SUB = 16
HG_G = 2
HG_SB = 8
FL_TQ = 512
DEC_P = 16
MOE_TM = 256
MOE_RB = 6
MOE_TF = 512
GATHER_T = 256
COMBINE_T = 128


def _cparams(sem):
    return pltpu.CompilerParams(dimension_semantics=sem, vmem_limit_bytes=VMEM_LIMIT_BYTES)


def _fit_tile(n, t, align=LANES):
    t = min(t, n) // align * align
    while n % t:
        t -= align
    return t


def _dot(a, b, dims, precision=None):
    return lax.dot_general(a, b, (dims, ((), ())), precision=precision,
                           preferred_element_type=F32)


def _dot_nn(a, b, precision=None):
    return _dot(a, b, ((1,), (0,)), precision)


def _dot_nt(a, b, precision=None):
    return _dot(a, b, ((1,), (1,)), precision)


def _dot_tn(a, b, precision=None):
    return _dot(a, b, ((0,), (0,)), precision)


def _sigmoid(x):
    return 1.0 / (1.0 + jnp.exp(-x))


def _mm_kernel(*refs, has_bias, silu_in, m_axis):
    x_ref, w_ref = refs[0], refs[1]
    b_ref = refs[2] if has_bias else None
    o_ref, wbf = refs[-2], refs[-1]

    @pl.when(pl.program_id(m_axis) == 0)
    def _():
        wbf[...] = w_ref[...].astype(BF16)

    x = x_ref[...]
    if silu_in:
        x = x.astype(F32)
        x = x * _sigmoid(x)
    acc = _dot_nn(x.astype(BF16), wbf[...])
    if has_bias:
        acc = acc + b_ref[...]
    o_ref[...] = acc.astype(o_ref.dtype)


def _mm(x, w, bias=None, *, layer=None, out_dtype=F32, tm=MM_TM, tn=512, silu_in=False,
        name="mm"):
    M, K = x.shape
    N = w.shape[-1]
    tm = _fit_tile(M, tm, 8)
    tn = _fit_tile(N, tn)
    assert M % tm == 0 and N % tn == 0 and w.shape[-2] == K
    batched = w.ndim == 3 and layer is None
    has_bias = bias is not None
    if w.ndim == 2:
        w = w[None]
        layer = 0
    if has_bias:
        bias = bias.reshape(w.shape[0], 1, N)
    if batched:
        grid = (w.shape[0], N // tn, M // tm)
        x_spec = pl.BlockSpec((tm, K), lambda l, n, m: (m, 0))
        w_spec = pl.BlockSpec((None, K, tn), lambda l, n, m: (l, 0, n))
        b_spec = pl.BlockSpec((None, 1, tn), lambda l, n, m: (l, 0, n))
        o_spec = pl.BlockSpec((None, tm, tn), lambda l, n, m: (l, m, n))
        out_shape = jax.ShapeDtypeStruct((w.shape[0], M, N), out_dtype)
        sem = ("arbitrary", "arbitrary", "arbitrary")
        m_axis = 2
    else:
        grid = (N // tn, M // tm)
        x_spec = pl.BlockSpec((tm, K), lambda n, m: (m, 0))
        w_spec = pl.BlockSpec((None, K, tn), lambda n, m: (layer, 0, n))
        b_spec = pl.BlockSpec((None, 1, tn), lambda n, m: (layer, 0, n))
        o_spec = pl.BlockSpec((tm, tn), lambda n, m: (m, n))
        out_shape = jax.ShapeDtypeStruct((M, N), out_dtype)
        sem = ("arbitrary", "arbitrary")
        m_axis = 1
    in_specs = [x_spec, w_spec] + ([b_spec] if has_bias else [])
    args = (x, w) + ((bias,) if has_bias else ())
    return pl.pallas_call(
        functools.partial(_mm_kernel, has_bias=has_bias, silu_in=silu_in, m_axis=m_axis),
        grid=grid, in_specs=in_specs, out_specs=o_spec, out_shape=out_shape,
        scratch_shapes=[pltpu.VMEM((K, tn), BF16)],
        compiler_params=_cparams(sem), name=name)(*args)


def _topk_softmax(logits, k):
    tm, n_e = logits.shape
    lane = lax.broadcasted_iota(I32, logits.shape, 1).astype(F32)
    l = logits
    vals, idxs = [], []
    for _ in range(k):
        m = jnp.max(l, axis=-1, keepdims=True)
        idx = jnp.min(jnp.where(l == m, lane, float(n_e)), axis=-1, keepdims=True)
        vals.append(m)
        idxs.append(idx)
        l = jnp.where(lane == idx, -jnp.inf, l)
    es = [jnp.exp(v - vals[0]) for v in vals]
    tot = es[0]
    for e in es[1:]:
        tot = tot + e
    k_lane = lax.broadcasted_iota(I32, (tm, k), 1)
    idx_out = jnp.zeros((tm, k), F32)
    gate_out = jnp.zeros((tm, k), F32)
    for j in range(k):
        idx_out = jnp.where(k_lane == j, idxs[j], idx_out)
        gate_out = jnp.where(k_lane == j, es[j] / tot, gate_out)
    return idx_out.astype(I32), gate_out


def _norm_kernel(*refs, n_p_tiles, alpha, first, has_h, has_router, top_k):
    it = iter(refs)
    x_ref = next(it)
    is_p = pl.program_id(0) < n_p_tiles
    x = x_ref[...]
    if first:
        xn = x
    else:
        y_ref, gp, gs, g_ref, b_ref = next(it), next(it), next(it), next(it), next(it)
        gate = jnp.where(is_p, gp[...], gs[...])
        z = alpha * x + (1.0 + gate) * y_ref[...].astype(F32)
        mu = jnp.mean(z, axis=-1, keepdims=True)
        zc = z - mu
        var = jnp.mean(zc * zc, axis=-1, keepdims=True)
        xn = zc * lax.rsqrt(var + NORM_EPS) * g_ref[...] + b_ref[...]
    if has_h:
        sp, ss, hp, hs = next(it), next(it), next(it), next(it)
        scale = jnp.where(is_p, sp[...], ss[...])
        shift = jnp.where(is_p, hp[...], hs[...])
        h = xn * (1.0 + scale) + shift
    if has_router:
        wr, br = next(it), next(it)
    if not first:
        xn_ref = next(it)
        xn_ref[...] = xn
    if has_h:
        h_ref = next(it)
        h_ref[...] = h.astype(h_ref.dtype)
    if has_router:
        idx_ref, gate_ref = next(it), next(it)
        logits = _dot_nn(h, wr[...], HIGHEST) + br[...]
        idx, gates = _topk_softmax(logits, top_k)
        idx_ref[...] = idx
        gate_ref[...] = gates


def _norm_mod(x, y, mods_p, mods_s, *, seq_len, n_prompt, gate_k=None, ln_g=None, ln_b=None,
              mod_k=None, h_mods=None, h_dtype=BF16, router=None, alpha=1.0, name="norm"):
    N, D = x.shape
    tm = LN_TM
    assert N % tm == 0 and n_prompt % tm == 0 and seq_len % tm == 0
    n_p_tiles = n_prompt // tm
    n_seq = mods_p.shape[0]
    first = y is None
    has_h = mod_k is not None
    has_router = router is not None

    def p_spec(k):
        return pl.BlockSpec((None, 1, D),
                            lambda i: (jnp.minimum(i * tm // seq_len, n_seq - 1), 0, k))

    def s_spec(k):
        return pl.BlockSpec((tm, D), lambda i: (jnp.maximum(i - n_p_tiles, 0), k))

    row_spec = pl.BlockSpec((tm, D), lambda i: (i, 0))
    vec_spec = pl.BlockSpec((1, D), lambda i: (0, 0))
    args, in_specs = [x], [row_spec]
    if not first:
        args += [y, mods_p, mods_s, ln_g.reshape(1, D), ln_b.reshape(1, D)]
        in_specs += [row_spec, p_spec(gate_k), s_spec(gate_k), vec_spec, vec_spec]
    if has_h:
        shift_k, scale_k = mod_k
        hm_p, hm_s = h_mods if h_mods is not None else (mods_p, mods_s)
        args += [hm_p, hm_s, hm_p, hm_s]
        in_specs += [p_spec(scale_k), s_spec(scale_k), p_spec(shift_k), s_spec(shift_k)]
    out_shape, out_specs = [], []
    if has_router:
        w_r, b_r = router
        n_e = w_r.shape[-1]
        args += [w_r, b_r.reshape(1, n_e)]
        in_specs += [pl.BlockSpec((D, n_e), lambda i: (0, 0)),
                     pl.BlockSpec((1, n_e), lambda i: (0, 0))]
    if not first:
        out_shape.append(jax.ShapeDtypeStruct((N, D), F32))
        out_specs.append(row_spec)
    if has_h:
        out_shape.append(jax.ShapeDtypeStruct((N, D), h_dtype))
        out_specs.append(row_spec)
    if has_router:
        out_shape += [jax.ShapeDtypeStruct((N, TOP_K), I32), jax.ShapeDtypeStruct((N, TOP_K), F32)]
        out_specs += [pl.BlockSpec((tm, TOP_K), lambda i: (i, 0))] * 2
    return pl.pallas_call(
        functools.partial(_norm_kernel, n_p_tiles=n_p_tiles, alpha=alpha, first=first,
                          has_h=has_h, has_router=has_router, top_k=TOP_K),
        grid=(N // tm,), in_specs=in_specs, out_specs=out_specs, out_shape=out_shape,
        compiler_params=_cparams(("arbitrary",)), name=name)(*args)


def _hgrn_unit(qr, fr, ir, gr, lb, ng, s_prev, sub, kpad, bpad, vpad):
    tb = qr.shape[0]
    ns = tb // sub
    q = qr * _sigmoid(qr)
    forget = lb + (1.0 - lb) * _sigmoid(fr)
    lf = jnp.log(jnp.maximum(forget, FORGET_FLOOR))
    kin = (1.0 - lb) * _sigmoid(-fr)
    v = ir

    row = lax.broadcasted_iota(I32, (tb, tb), 0)
    col = lax.broadcasted_iota(I32, (tb, tb), 1)
    tri = jnp.where(col <= row, 1.0, 0.0).astype(F32)
    cum = _dot_nn(tri, lf, HIGHEST)
    if ns > 1:
        prev = jnp.where(col < (row // sub) * sub, 1.0, 0.0).astype(F32)
        ref = _dot_nn(prev, lf, HIGHEST)
        loc = cum - ref
    else:
        ref = None
        loc = cum
    ones = jnp.ones((tb, HEAD), F32)
    tot_col = _dot_tn(lf, ones, HIGHEST)

    o = _dot_nn((q * jnp.exp(cum)).astype(BF16), s_prev.astype(BF16))

    kpad[pl.ds(sub, tb), :] = kin
    bpad[pl.ds(sub, tb), :] = loc
    vpad[pl.ds(sub, tb), :] = v
    pos = lax.broadcasted_iota(I32, (tb, 1), 0) % sub
    for d in range(sub):
        k_sh = kpad[pl.ds(sub - d, tb), :]
        b_sh = bpad[pl.ds(sub - d, tb), :]
        v_sh = vpad[pl.ds(sub - d, tb), :]
        p = q * k_sh * jnp.exp(jnp.minimum(loc - b_sh, 0.0))
        a = jnp.sum(p, axis=-1, keepdims=True)
        o = o + jnp.where(pos >= d, a, 0.0) * v_sh

    if ns > 1:
        parts = [o[:sub]]
        for i in range(1, ns):
            lo = i * sub
            q_t = (q[lo:lo + sub] * jnp.exp(loc[lo:lo + sub])).astype(BF16)
            k_t = (kin[:lo] * jnp.exp(jnp.minimum(ref[lo:lo + 1] - cum[:lo], 0.0))).astype(BF16)
            att = _dot_nt(q_t, k_t)
            parts.append(o[lo:lo + sub] + _dot_nn(att.astype(BF16), v[:lo].astype(BF16)))
        o = jnp.concatenate(parts, axis=0)

    k_e = (kin * jnp.exp(cum[tb - 1:tb] - cum)).astype(BF16)
    s_new = jnp.exp(tot_col) * s_prev + _dot_tn(k_e, v.astype(BF16))

    o = o * lax.rsqrt(jnp.mean(o * o, axis=-1, keepdims=True) + NORM_EPS) * ng
    o = o * _sigmoid(gr)
    return o, s_new


def _hgrn_prompt_kernel(q_ref, f_ref, i_ref, g_ref, lb_ref, ng_ref, o_ref, s_out, state,
                        kpad, bpad, vpad, *, n_heads, sub):
    c = pl.program_id(2)

    @pl.when(c == 0)
    def _():
        state[...] = jnp.zeros_like(state)
        kpad[...] = jnp.zeros_like(kpad)
        bpad[...] = jnp.zeros_like(bpad)
        vpad[...] = jnp.zeros_like(vpad)

    for h in range(n_heads):
        sl = slice(h * HEAD, (h + 1) * HEAD)
        o, s_new = _hgrn_unit(q_ref[:, sl], f_ref[:, sl], i_ref[:, sl], g_ref[:, sl],
                              lb_ref[:, sl], ng_ref[...], state[h], sub, kpad, bpad, vpad)
        o_ref[:, sl] = o.astype(o_ref.dtype)
        state[h] = s_new

    @pl.when(c == pl.num_programs(2) - 1)
    def _():
        s_out[...] = state[...]


def _hgrn_prompt(proj, lb, ng, *, batch, seq_len, row0=0):
    hf = proj.shape[1] // 4
    n_h = hf // HEAD
    g, tb, sub = HG_G, HG_TB, HG_SUB
    assert n_h % g == 0 and seq_len % tb == 0 and row0 % tb == 0
    nc = seq_len // tb
    ncol = hf // (g * HEAD)
    r0 = row0 // tb

    def seg_spec(seg):
        return pl.BlockSpec((tb, g * HEAD), lambda b, hg, c: (r0 + b * nc + c, seg * ncol + hg))

    return pl.pallas_call(
        functools.partial(_hgrn_prompt_kernel, n_heads=g, sub=sub),
        grid=(batch, n_h // g, nc),
        in_specs=[seg_spec(0), seg_spec(1), seg_spec(2), seg_spec(3),
                  pl.BlockSpec((1, g * HEAD), lambda b, hg, c: (0, hg)),
                  pl.BlockSpec((1, HEAD), lambda b, hg, c: (0, 0))],
        out_specs=[pl.BlockSpec((tb, g * HEAD), lambda b, hg, c: (b * nc + c, hg)),
                   pl.BlockSpec((None, g, HEAD, HEAD), lambda b, hg, c: (b, hg, 0, 0))],
        out_shape=[jax.ShapeDtypeStruct((batch * seq_len, hf), BF16),
                   jax.ShapeDtypeStruct((batch, n_h, HEAD, HEAD), F32)],
        scratch_shapes=[pltpu.VMEM((g, HEAD, HEAD), F32)]
        + [pltpu.VMEM((sub + tb, HEAD), F32)] * 3,
        compiler_params=_cparams(("arbitrary", "arbitrary", "arbitrary")),
        name="hgrn_prompt")(proj, proj, proj, proj, lb.reshape(1, hf), ng.reshape(1, HEAD))


def _hgrn_sample_kernel(q_ref, f_ref, i_ref, g_ref, lb_ref, ng_ref, s_ref, o_ref, s_out,
                        kpad, bpad, vpad, *, n_seq, n_heads, t):
    kpad[...] = jnp.zeros_like(kpad)
    bpad[...] = jnp.zeros_like(bpad)
    vpad[...] = jnp.zeros_like(vpad)
    for s in range(n_seq):
        rows = slice(s * t, (s + 1) * t)
        for h in range(n_heads):
            sl = slice(h * HEAD, (h + 1) * HEAD)
            o, s_new = _hgrn_unit(q_ref[rows, sl], f_ref[rows, sl], i_ref[rows, sl],
                                  g_ref[rows, sl], lb_ref[:, sl], ng_ref[...], s_ref[s, h],
                                  t, kpad, bpad, vpad)
            o_ref[rows, sl] = o.astype(o_ref.dtype)
            s_out[s, h] = s_new


def _hgrn_sample(proj, lb, ng, s0, *, t, row0):
    hf = proj.shape[1] // 4
    n_h = hf // HEAD
    n_seq = s0.shape[0]
    sb, g = HG_SB, HG_G
    rb = sb * t
    assert n_seq % sb == 0 and n_h % g == 0 and row0 % rb == 0
    ncol = hf // (g * HEAD)
    r0 = row0 // rb

    def seg_spec(seg):
        return pl.BlockSpec((rb, g * HEAD), lambda s, hg: (r0 + s, seg * ncol + hg))

    return pl.pallas_call(
        functools.partial(_hgrn_sample_kernel, n_seq=sb, n_heads=g, t=t),
        grid=(n_seq // sb, n_h // g),
        in_specs=[seg_spec(0), seg_spec(1), seg_spec(2), seg_spec(3),
                  pl.BlockSpec((1, g * HEAD), lambda s, hg: (0, hg)),
                  pl.BlockSpec((1, HEAD), lambda s, hg: (0, 0)),
                  pl.BlockSpec((sb, g, HEAD, HEAD), lambda s, hg: (s, hg, 0, 0))],
        out_specs=[pl.BlockSpec((rb, g * HEAD), lambda s, hg: (s, hg)),
                   pl.BlockSpec((sb, g, HEAD, HEAD), lambda s, hg: (s, hg, 0, 0))],
        out_shape=[jax.ShapeDtypeStruct((n_seq * t, hf), BF16),
                   jax.ShapeDtypeStruct(s0.shape, F32)],
        scratch_shapes=[pltpu.VMEM((2 * t, HEAD), F32)] * 3,
        compiler_params=_cparams(("arbitrary", "arbitrary")),
        name="hgrn_sample")(proj, proj, proj, proj, lb.reshape(1, hf), ng.reshape(1, HEAD), s0)


def _rope128(x, cos_t, sin_a, sin_b):
    return (x * cos_t + pltpu.roll(x, 96, axis=1) * sin_a + pltpu.roll(x, 32, axis=1) * sin_b)


def _mla_post_kernel(a_ref, qg_ref, kg_ref, cos_ref, sa_ref, sb_ref,
                     cq_ref, ckv_ref, kr_ref, krb_ref):
    a = a_ref[...]
    cq = a[:, :Q_LORA]
    cq = cq * lax.rsqrt(jnp.mean(cq * cq, axis=-1, keepdims=True) + NORM_EPS) * qg_ref[...]
    cq_ref[...] = cq.astype(cq_ref.dtype)
    ckv = a[:, Q_LORA:Q_LORA + KV_LORA]
    ckv = ckv * lax.rsqrt(jnp.mean(ckv * ckv, axis=-1, keepdims=True) + NORM_EPS) * kg_ref[...]
    ckv_ref[...] = ckv
    kr = _rope128(a[:, Q_LORA + KV_LORA:], cos_ref[...], sa_ref[...], sb_ref[...])
    kr_ref[...] = kr
    krb_ref[...] = kr.astype(krb_ref.dtype)


def _mla_post(a, q_g, kv_g, cos_t, sin_a, sin_b):
    n = a.shape[0]
    tm = LN_TM
    row = lambda w: pl.BlockSpec((tm, w), lambda i: (i, 0))
    vec = lambda w: pl.BlockSpec((1, w), lambda i: (0, 0))
    return pl.pallas_call(
        _mla_post_kernel, grid=(n // tm,),
        in_specs=[row(a.shape[1]), vec(Q_LORA), vec(KV_LORA), row(LANES), row(LANES), row(LANES)],
        out_specs=[row(Q_LORA), row(KV_LORA), row(LANES), row(LANES)],
        out_shape=[jax.ShapeDtypeStruct((n, Q_LORA), BF16),
                   jax.ShapeDtypeStruct((n, KV_LORA), F32),
                   jax.ShapeDtypeStruct((n, LANES), F32),
                   jax.ShapeDtypeStruct((n, LANES), BF16)],
        compiler_params=_cparams(("arbitrary",)), name="mla_post")(
            a, q_g.reshape(1, Q_LORA), kv_g.reshape(1, KV_LORA), cos_t, sin_a, sin_b)


def _q_rope_kernel(q_ref, cos_ref, sa_ref, sb_ref, o_ref):
    x = q_ref[...]
    r = _rope128(x[:, HEAD:], cos_ref[...], sa_ref[...], sb_ref[...])
    o_ref[...] = jnp.concatenate([x[:, :HEAD], r], axis=1).astype(o_ref.dtype)


def _q_rope(q, cos_t, sin_a, sin_b):
    n, w = q.shape
    tm = LN_TM
    tab = pl.BlockSpec((tm, LANES), lambda i, h: (i, 0))
    blk = pl.BlockSpec((tm, 2 * HEAD), lambda i, h: (i, h))
    return pl.pallas_call(
        _q_rope_kernel, grid=(n // tm, w // (2 * HEAD)),
        in_specs=[blk, tab, tab, tab], out_specs=blk,
        out_shape=jax.ShapeDtypeStruct((n, w), BF16),
        compiler_params=_cparams(("arbitrary", "arbitrary")), name="q_rope")(q, cos_t, sin_a, sin_b)


def _flash_kernel(q_ref, kv_ref, kr_ref, o_ref, m_sc, l_sc, acc_sc, *, scale, tq, tk):
    qi, ki = pl.program_id(2), pl.program_id(3)

    @pl.when(ki == 0)
    def _():
        m_sc[...] = jnp.full_like(m_sc, MASK_VALUE)
        l_sc[...] = jnp.zeros_like(l_sc)
        acc_sc[...] = jnp.zeros_like(acc_sc)

    @pl.when(ki <= qi)
    def _():
        kv = kv_ref[...]
        k = jnp.concatenate([kv[:, :HEAD], kr_ref[...]], axis=1)
        s = _dot_nt(q_ref[...], k) * scale
        qpos = qi * tq + lax.broadcasted_iota(I32, (tq, tk), 0)
        kpos = ki * tk + lax.broadcasted_iota(I32, (tq, tk), 1)
        s = jnp.where(kpos <= qpos, s, MASK_VALUE)
        m_prev = m_sc[...]
        m_new = jnp.maximum(m_prev, jnp.max(s, axis=-1, keepdims=True))
        alpha = jnp.exp(m_prev - m_new)
        p = jnp.exp(s - m_new)
        l_sc[...] = alpha * l_sc[...] + jnp.sum(p, axis=-1, keepdims=True)
        acc_sc[...] = alpha * acc_sc[...] + _dot_nn(p.astype(BF16), kv[:, HEAD:])
        m_sc[...] = m_new

    @pl.when(ki == pl.num_programs(3) - 1)
    def _():
        o_ref[...] = (acc_sc[...] / l_sc[...]).astype(o_ref.dtype)


def _flash(q, kv, krb, *, batch, seq_len, scale):
    n_h = q.shape[1] // (2 * HEAD)
    tq = tk = min(FL_TQ, seq_len)
    nq = seq_len // tq
    return pl.pallas_call(
        functools.partial(_flash_kernel, scale=scale, tq=tq, tk=tk),
        grid=(batch, n_h, nq, nq),
        in_specs=[pl.BlockSpec((tq, 2 * HEAD), lambda b, h, qi, ki: (b * nq + qi, h)),
                  pl.BlockSpec((tk, 2 * HEAD),
                               lambda b, h, qi, ki: (b * nq + jnp.minimum(ki, qi), h)),
                  pl.BlockSpec((tk, LANES),
                               lambda b, h, qi, ki: (b * nq + jnp.minimum(ki, qi), 0))],
        out_specs=pl.BlockSpec((tq, HEAD), lambda b, h, qi, ki: (b * nq + qi, h)),
        out_shape=jax.ShapeDtypeStruct((batch * seq_len, n_h * HEAD), BF16),
        scratch_shapes=[pltpu.VMEM((tq, 1), F32), pltpu.VMEM((tq, 1), F32),
                        pltpu.VMEM((tq, HEAD), F32)],
        compiler_params=_cparams(("arbitrary",) * 4), name="mla_flash")(q, kv, krb)


def _q_absorb_kernel(q_ref, w_ref, ql_ref, qr_ref, *, t):
    x = q_ref[...]
    n = x.shape[0]
    ql = _dot_nt(x[:, :HEAD], w_ref[...].astype(BF16))
    ql_ref[...] = ql.reshape(n // t, t, KV_LORA)
    qr_ref[...] = x[:, HEAD:].astype(F32).reshape(n // t, t, HEAD)


def _q_absorb(q_s, w_kv_up, layer, *, t):
    ns, w = q_s.shape
    n_h = w // (2 * HEAD)
    n_seq = ns // t
    return pl.pallas_call(
        functools.partial(_q_absorb_kernel, t=t), grid=(n_h,),
        in_specs=[pl.BlockSpec((ns, 2 * HEAD), lambda h: (0, h)),
                  pl.BlockSpec((None, KV_LORA, HEAD), lambda h: (layer, 0, 2 * h))],
        out_specs=[pl.BlockSpec((n_seq, None, t, KV_LORA), lambda h: (0, h, 0, 0)),
                   pl.BlockSpec((n_seq, None, t, HEAD), lambda h: (0, h, 0, 0))],
        out_shape=[jax.ShapeDtypeStruct((n_seq, n_h, t, KV_LORA), F32),
                   jax.ShapeDtypeStruct((n_seq, n_h, t, HEAD), F32)],
        compiler_params=_cparams(("arbitrary",)), name="q_absorb")(q_s, w_kv_up)


def _decode_kernel(pt_ref, ql_ref, qr_ref, cn_ref, kn_ref, lat_hbm, kr_hbm, o_ref,
                   lat_buf, kr_buf, sem, m_sc, l_sc, acc_sc, *, layer, scale, n_pg, t):
    j = pl.program_id(1)
    nj = pl.num_programs(1)
    step = pl.program_id(0) * nj + j
    n_steps = pl.num_programs(0) * nj
    slot = step % 2

    def page_copies(stp, slt):
        out = []
        for i in range(n_pg):
            pg = pt_ref[stp * n_pg + i]
            out.append(pltpu.make_async_copy(lat_hbm.at[layer, pg], lat_buf.at[slt, i],
                                             sem.at[0, slt]))
            out.append(pltpu.make_async_copy(kr_hbm.at[layer, pg], kr_buf.at[slt, i],
                                             sem.at[1, slt]))
        return out

    @pl.when(step == 0)
    def _():
        for c in page_copies(0, 0):
            c.start()

    @pl.when(step + 1 < n_steps)
    def _():
        for c in page_copies(step + 1, 1 - slot):
            c.start()

    for c in page_copies(step, slot):
        c.wait()

    @pl.when(j == 0)
    def _():
        m_sc[...] = jnp.full_like(m_sc, MASK_VALUE)
        l_sc[...] = jnp.zeros_like(l_sc)
        acc_sc[...] = jnp.zeros_like(acc_sc)

    q = ql_ref[...].astype(BF16)
    qr = qr_ref[...][:, :QK_ROPE].astype(BF16)

    def update(s, values):
        m_prev = m_sc[...]
        m_new = jnp.maximum(m_prev, jnp.max(s, axis=-1, keepdims=True))
        alpha = jnp.exp(m_prev - m_new)
        p = jnp.exp(s - m_new)
        l_sc[...] = alpha * l_sc[...] + jnp.sum(p, axis=-1, keepdims=True)
        acc = alpha * acc_sc[...]
        w = 0
        for val in values:
            n = val.shape[0]
            acc = acc + _dot_nn(p[:, w:w + n].astype(BF16), val)
            w += n
        acc_sc[...] = acc
        m_sc[...] = m_new

    lats = [lat_buf[slot, i].astype(BF16) for i in range(n_pg)]
    s = jnp.concatenate(
        [_dot_nt(q, lats[i]) + _dot_nt(qr, kr_buf[slot, i].astype(BF16)) for i in range(n_pg)],
        axis=1) * scale
    update(s, lats)

    @pl.when(j == nj - 1)
    def _():
        cn = cn_ref[...].astype(BF16)
        kn = kn_ref[...][:, :QK_ROPE].astype(BF16)
        s_new = (_dot_nt(q, cn) + _dot_nt(qr, kn)) * scale
        rows = s_new.shape[0]
        qpos = lax.broadcasted_iota(I32, (rows, t), 0) % t
        kpos = lax.broadcasted_iota(I32, (rows, t), 1)
        update(jnp.where(kpos <= qpos, s_new, MASK_VALUE), [cn])
        o_ref[...] = acc_sc[...] / l_sc[...]


def _decode(q_lat, q_rp, ckv_new, kr_new, cache_lat, cache_kr, page_table, layer, *, scale):
    n_seq, rows, _ = q_lat.shape
    t = ckv_new.shape[1]
    n_pages = page_table.shape[1]
    page = cache_lat.shape[2]
    n_pg = min(DEC_P, n_pages)
    assert n_pages % n_pg == 0

    seq3 = lambda r, w: pl.BlockSpec((None, r, w), lambda b, j, pt: (b, 0, 0))
    hbm = pl.BlockSpec(memory_space=pl.ANY)
    grid_spec = pltpu.PrefetchScalarGridSpec(
        num_scalar_prefetch=1, grid=(n_seq, n_pages // n_pg),
        in_specs=[seq3(rows, KV_LORA), seq3(rows, HEAD), seq3(t, KV_LORA), seq3(t, HEAD),
                  hbm, hbm],
        out_specs=seq3(rows, KV_LORA),
        scratch_shapes=[pltpu.VMEM((2, n_pg, page, KV_LORA), F32),
                        pltpu.VMEM((2, n_pg, page, QK_ROPE), F32),
                        pltpu.SemaphoreType.DMA((2, 2)),
                        pltpu.VMEM((rows, 1), F32), pltpu.VMEM((rows, 1), F32),
                        pltpu.VMEM((rows, KV_LORA), F32)])
    return pl.pallas_call(
        functools.partial(_decode_kernel, layer=layer, scale=scale, n_pg=n_pg, t=t),
        grid_spec=grid_spec,
        out_shape=jax.ShapeDtypeStruct((n_seq, rows, KV_LORA), F32),
        compiler_params=_cparams(("arbitrary", "arbitrary")), name="mla_decode")(
            page_table.reshape(-1), q_lat, q_rp, ckv_new, kr_new, cache_lat, cache_kr)


def _o_up_kernel(x_ref, w_ref, o_ref):
    x = x_ref[...]
    n_seq, t, c = x.shape
    o = _dot_nn(x.reshape(n_seq * t, c).astype(BF16), w_ref[...].astype(BF16))
    o_ref[...] = o.astype(o_ref.dtype)


def _o_up(o_lat, w_kv_up, layer):
    n_seq, n_h, t, c = o_lat.shape
    return pl.pallas_call(
        _o_up_kernel, grid=(n_h,),
        in_specs=[pl.BlockSpec((n_seq, None, t, c), lambda h: (0, h, 0, 0)),
                  pl.BlockSpec((None, c, HEAD), lambda h: (layer, 0, 2 * h + 1))],
        out_specs=pl.BlockSpec((n_seq * t, HEAD), lambda h: (0, h)),
        out_shape=jax.ShapeDtypeStruct((n_seq * t, n_h * HEAD), BF16),
        compiler_params=_cparams(("arbitrary",)), name="o_up")(o_lat, w_kv_up)


def _gather_kernel(idx_ref, src_ref, o_ref, buf, sem, *, n_rows):
    def row_copy(r):
        return pltpu.make_async_copy(src_ref.at[pl.ds(idx_ref[0, r], 1)], buf.at[pl.ds(r, 1)], sem)

    def start(r, c):
        row_copy(r).start()
        return c

    def wait(r, c):
        row_copy(r).wait()
        return c

    lax.fori_loop(0, n_rows, start, 0)
    lax.fori_loop(0, n_rows, wait, 0)
    o_ref[...] = buf[...].astype(o_ref.dtype)


def _gather_rows(src, idx, *, out_dtype=BF16):
    n = idx.shape[0]
    d = src.shape[1]
    t = GATHER_T
    assert n % t == 0
    return pl.pallas_call(
        functools.partial(_gather_kernel, n_rows=t), grid=(n // t,),
        in_specs=[pl.BlockSpec((None, 1, t), lambda i: (i, 0, 0), memory_space=pltpu.SMEM),
                  pl.BlockSpec(memory_space=pl.ANY)],
        out_specs=pl.BlockSpec((t, d), lambda i: (i, 0)),
        out_shape=jax.ShapeDtypeStruct((n, d), out_dtype),
        scratch_shapes=[pltpu.VMEM((t, d), F32), pltpu.SemaphoreType.DMA(())],
        compiler_params=_cparams(("arbitrary",)), name="moe_gather")(idx.reshape(n // t, 1, t), src)


def _combine_kernel(idx_ref, gate_ref, src_ref, o_ref, buf, sem, *, n_tok, k):
    def row_copy(r):
        return pltpu.make_async_copy(src_ref.at[pl.ds(idx_ref[0, r], 1)],
                                     buf.at[r % k, pl.ds(r // k, 1)], sem)

    def start(r, c):
        row_copy(r).start()
        return c

    def wait(r, c):
        row_copy(r).wait()
        return c

    lax.fori_loop(0, n_tok * k, start, 0)
    lax.fori_loop(0, n_tok * k, wait, 0)
    g = gate_ref[...]
    acc = g[:, 0:1] * buf[0]
    for j in range(1, k):
        acc = acc + g[:, j:j + 1] * buf[j]
    o_ref[...] = acc


def _combine(yb, dest, gates):
    n, k = dest.shape
    d = yb.shape[1]
    t = COMBINE_T
    assert n % t == 0
    return pl.pallas_call(
        functools.partial(_combine_kernel, n_tok=t, k=k), grid=(n // t,),
        in_specs=[pl.BlockSpec((None, 1, t * k), lambda i: (i, 0, 0), memory_space=pltpu.SMEM),
                  pl.BlockSpec((t, k), lambda i: (i, 0)),
                  pl.BlockSpec(memory_space=pl.ANY)],
        out_specs=pl.BlockSpec((t, d), lambda i: (i, 0)),
        out_shape=jax.ShapeDtypeStruct((n, d), F32),
        scratch_shapes=[pltpu.VMEM((k, t, d), F32), pltpu.SemaphoreType.DMA(())],
        compiler_params=_cparams(("arbitrary",)), name="moe_combine")(
            dest.reshape(n // t, 1, t * k), gates, yb)


def _moe_ffn_kernel(se_ref, sb_ref, sn_ref, sz_ref, x_ref, wg_ref, wu_ref, wd_ref, bg_ref,
                    bu_ref, bd_ref, o_ref, wg_bf, wu_bf, wd_bf, acc):
    s, j, r = pl.program_id(0), pl.program_id(1), pl.program_id(2)
    nj = pl.num_programs(1)
    n_sub = sn_ref[s]

    @pl.when(jnp.logical_and(j == nj - 1, r < sz_ref[s]))
    def _():
        o_ref[...] = jnp.zeros_like(o_ref)

    @pl.when(jnp.logical_and(r == 0, n_sub > 0))
    def _():
        wg_bf[...] = wg_ref[...].astype(BF16)
        wu_bf[...] = wu_ref[...].astype(BF16)
        wd_bf[...] = wd_ref[...].astype(BF16)

    @pl.when(r < n_sub)
    def _():
        x = x_ref[...]
        gate = jnp.minimum(_dot_nn(x, wg_bf[...]) + bg_ref[...], SWIGLU_LIMIT)
        up = jnp.clip(_dot_nn(x, wu_bf[...]) + bu_ref[...], -SWIGLU_LIMIT, SWIGLU_LIMIT)
        act = (up + 1.0) * (gate * _sigmoid(SWIGLU_ALPHA * gate))
        part = _dot_nn(act.astype(BF16), wd_bf[...])

        @pl.when(j == 0)
        def _():
            acc[r] = part

        @pl.when(jnp.logical_and(j > 0, j < nj - 1))
        def _():
            acc[r] = acc[r] + part

        @pl.when(j == nj - 1)
        def _():
            o_ref[...] = acc[r] + part + bd_ref[...]


def _moe_ffn(xb, w_gu, b_gu, w_down, b_down, layer, sb_expert, sb_block, sb_nsub, sb_nzero):
    d = xb.shape[1]
    tm, rb, tf = MOE_TM, MOE_RB, MOE_TF
    n_blocks = xb.shape[0] // tm
    d_ff = w_down.shape[2]
    n_e = w_down.shape[1]
    nj = d_ff // tf
    n_super = sb_expert.shape[0]
    assert nj >= 2

    def jeff(s, j, sn):
        return jnp.where(sn[s] > 0, j, nj - 1)

    def x_map(s, j, r, se, sb, sn, sz):
        return (sb[s] + jnp.minimum(r, jnp.maximum(sn[s] - 1, 0)), 0)

    def o_map(s, j, r, se, sb, sn, sz):
        cnt = jnp.where(sn[s] > 0, sn[s], sz[s])
        r_last = jnp.minimum(r, jnp.maximum(cnt - 1, 0))
        return (sb[s] + jnp.where(j == nj - 1, r_last, 0), 0)

    def w_spec(shape, fn):
        return pl.BlockSpec(shape, lambda s, j, r, se, sb, sn, sz: fn(se[s], jeff(s, j, sn)))

    grid_spec = pltpu.PrefetchScalarGridSpec(
        num_scalar_prefetch=4, grid=(n_super, nj, rb),
        in_specs=[
            pl.BlockSpec((tm, d), x_map),
            w_spec((None, None, d, tf), lambda e, jj: (layer, e, 0, jj)),
            w_spec((None, None, d, tf), lambda e, jj: (layer, e, 0, nj + jj)),
            w_spec((None, None, tf, d), lambda e, jj: (layer, e, jj, 0)),
            w_spec((None, None, 1, tf), lambda e, jj: (layer, e, 0, jj)),
            w_spec((None, None, 1, tf), lambda e, jj: (layer, e, 0, nj + jj)),
            w_spec((None, None, 1, d), lambda e, jj: (layer, e, 0, 0)),
        ],
        out_specs=pl.BlockSpec((tm, d), o_map),
        scratch_shapes=[pltpu.VMEM((d, tf), BF16), pltpu.VMEM((d, tf), BF16),
                        pltpu.VMEM((tf, d), BF16), pltpu.VMEM((rb, tm, d), F32)])
    n_l = w_gu.shape[0]
    return pl.pallas_call(
        _moe_ffn_kernel, grid_spec=grid_spec,
        out_shape=jax.ShapeDtypeStruct((n_blocks * tm, d), F32),
        compiler_params=_cparams(("arbitrary", "arbitrary", "arbitrary")), name="moe_ffn")(
            sb_expert, sb_block, sb_nsub, sb_nzero, xb, w_gu, w_gu, w_down,
            b_gu.reshape(n_l, n_e, 1, 2 * d_ff), b_gu.reshape(n_l, n_e, 1, 2 * d_ff),
            b_down.reshape(n_l, n_e, 1, d))


def _route(top_idx, n_e):
    n, k = top_idx.shape
    a = n * k
    tm, rb = MOE_TM, MOE_RB
    flat_e = top_idx.reshape(a)
    onehot = (flat_e[:, None] == jnp.arange(n_e, dtype=I32)[None, :]).astype(I32)
    csum = jnp.cumsum(onehot, axis=0)
    rank = jnp.sum((csum - 1) * onehot, axis=1)
    counts = csum[-1]
    nblk = (counts + tm - 1) // tm
    blk_end = jnp.cumsum(nblk)
    blk_start = blk_end - nblk
    dest = blk_start[flat_e] * tm + rank
    n_blocks = a // tm + n_e
    row_tok = jnp.zeros((n_blocks * tm,), I32).at[dest].set(jnp.arange(a, dtype=I32) // k)
    n_super = -(-n_blocks // rb) + n_e
    nsup = (nblk + rb - 1) // rb
    sup_end = jnp.cumsum(nsup)
    sup_start = sup_end - nsup
    sid = jnp.arange(n_super, dtype=I32)
    e_of = jnp.minimum(jnp.searchsorted(sup_end, sid, side="right"), n_e - 1).astype(I32)
    used = sid < sup_end[-1]
    local = sid - sup_start[e_of]
    sb_block = blk_start[e_of] + local * rb
    sb_nsub = jnp.clip(nblk[e_of] - local * rb, 0, rb)
    last = jnp.maximum(sup_end[-1] - 1, 0)
    sb_expert = jnp.where(used, e_of, e_of[last]).astype(I32)
    z_start = blk_end[-1] + (sid - sup_end[-1]) * rb
    sb_nzero = jnp.where(used, 0, jnp.clip(n_blocks - z_start, 0, rb)).astype(I32)
    sb_block = jnp.where(used, sb_block, jnp.minimum(z_start, n_blocks - 1)).astype(I32)
    sb_nsub = jnp.where(used, sb_nsub, 0).astype(I32)
    return row_tok, dest.reshape(n, k).astype(I32), sb_expert, sb_block, sb_nsub, sb_nzero


def _moe(h2, top_idx, gates, w_gu, b_gu, w_down, b_down, layer):
    n_e = w_down.shape[1]
    row_tok, dest, sb_e, sb_b, sb_n, sb_z = _route(top_idx, n_e)
    xb = _gather_rows(h2, row_tok)
    yb = _moe_ffn(xb, w_gu, b_gu, w_down, b_down, layer, sb_e, sb_b, sb_n, sb_z)
    return _combine(yb, dest, gates)


def _rope_tables(pos):
    half = QK_ROPE // 2
    inv = ROPE_THETA ** (-jnp.arange(half, dtype=F32) / half)
    ang = pos.astype(F32)[:, None] * inv[None, :]
    cos, sin = jnp.cos(ang), jnp.sin(ang)
    z = jnp.zeros_like(cos)
    cos_t = jnp.concatenate([cos, cos, z, z], axis=1)
    sin_a = jnp.concatenate([-sin, z, z, z], axis=1)
    sin_b = jnp.concatenate([z, sin, z, z], axis=1)
    return cos_t, sin_a, sin_b


def kernel(x_prompt, x_sample, c_prompt, c_sample, cache_mla_latent, cache_mla_krope, state_hgrn, page_table, ada_w, ada_b, ln_g, ln_b, hg_lb_logits, hg_w_in, hg_norm_g, hg_w_out, mla_w_in, mla_q_norm_g, mla_kv_norm_g, mla_w_q_up, mla_w_kv_up, mla_w_out, moe_w_router, moe_b_router, moe_w_gu, moe_b_gu, moe_w_down, moe_b_down):
    batch, seq_len, d = x_prompt.shape
    n_seq, t_s, _ = x_sample.shape
    depth = ada_w.shape[0]
    n_p, n_s = batch * seq_len, n_seq * t_s
    alpha = (2 * depth) ** 0.25
    past = page_table.shape[1] * cache_mla_latent.shape[2]
    n_h = mla_w_out.shape[1] // HEAD
    qk_dim = HEAD + QK_ROPE
    attn_scale = qk_dim ** -0.5

    x = jnp.concatenate([x_prompt.reshape(n_p, d), x_sample.reshape(n_s, d)], axis=0)

    n_c = batch + n_seq
    c_rows = -(-n_c // 8) * 8
    c_all = jnp.concatenate([c_prompt, c_sample, jnp.zeros((c_rows - n_c, d), F32)], axis=0)
    mods = _mm(c_all, ada_w, ada_b, out_dtype=F32, tm=c_rows, tn=1024, silu_in=True, name="adaln")
    mods_p = mods[:, :batch].reshape(depth, batch, 1, 6 * d)
    mods_s = jnp.repeat(mods[:, batch:n_c], t_s, axis=1)

    lb_p = jax.nn.softmax(hg_lb_logits.astype(F32), axis=0)
    lower_bounds = jnp.cumsum(lb_p, axis=0) - lb_p[0]

    pos = jnp.concatenate([jnp.tile(jnp.arange(seq_len), batch), jnp.tile(past + jnp.arange(t_s), n_seq)])
    cos_t, sin_a, sin_b = _rope_tables(pos)

    norm = functools.partial(_norm_mod, seq_len=seq_len, n_prompt=n_p)
    h = norm(x, None, mods_p[0], mods_s[0], mod_k=(0, 1), name="modulate0")[0]

    lat_p, kr_p, hg_p, lat_s, kr_s, hg_s = [], [], [], [], [], []
    for l in range(depth):
        j = l // 2
        mp, ms = mods_p[l], mods_s[l]
        if l % 2 == 0:
            proj = _mm(h, hg_w_in, layer=j, out_dtype=F32, tn=1024, name="hgrn_in")
            o_p, sp = _hgrn_prompt(proj, lower_bounds[j], hg_norm_g[j], batch=batch, seq_len=seq_len)
            o_s, ss = _hgrn_sample(proj, lower_bounds[j], hg_norm_g[j], state_hgrn[j], t=t_s, row0=n_p)
            hg_p.append(sp)
            hg_s.append(ss)
            y = _mm(jnp.concatenate([o_p, o_s], axis=0), hg_w_out, layer=j, out_dtype=F32,
                    tn=1024, name="hgrn_out")
        else:
            w_in = jnp.pad(mla_w_in[j], ((0, 0), (0, LANES - QK_ROPE)))
            a = _mm(h, w_in, out_dtype=F32, tn=w_in.shape[1], name="mla_in")
            cq, ckv, kr, krb = _mla_post(a, mla_q_norm_g[j], mla_kv_norm_g[j], cos_t, sin_a, sin_b)
            w_q = jnp.pad(mla_w_q_up[j].reshape(Q_LORA, n_h, qk_dim),
                          ((0, 0), (0, 0), (0, 2 * HEAD - qk_dim))).reshape(Q_LORA, n_h * 2 * HEAD)
            q = _q_rope(_mm(cq, w_q, out_dtype=F32, tn=1024, name="mla_q_up"), cos_t, sin_a, sin_b)
            kv = _mm(ckv[:n_p], mla_w_kv_up, layer=j, out_dtype=BF16, tn=1024, name="mla_kv_up")
            o_p = _flash(q[:n_p], kv, krb, batch=batch, seq_len=seq_len, scale=attn_scale)
            q_lat, q_rp = _q_absorb(q[n_p:], mla_w_kv_up, j, t=t_s)
            o_lat = _decode(q_lat.reshape(n_seq, n_h * t_s, KV_LORA),
                            q_rp.reshape(n_seq, n_h * t_s, HEAD),
                            ckv[n_p:].reshape(n_seq, t_s, KV_LORA),
                            kr[n_p:].reshape(n_seq, t_s, LANES),
                            cache_mla_latent, cache_mla_krope, page_table, j, scale=attn_scale)
            o_s = _o_up(o_lat.reshape(n_seq, n_h, t_s, KV_LORA), mla_w_kv_up, j)
            y = _mm(jnp.concatenate([o_p, o_s], axis=0), mla_w_out, layer=j, out_dtype=F32,
                    tn=1024, name="mla_out")
            lat_p.append(ckv[:n_p].reshape(batch, seq_len, KV_LORA))
            kr_p.append(kr[:n_p, :QK_ROPE].reshape(batch, seq_len, QK_ROPE))
            lat_s.append(ckv[n_p:].reshape(n_seq, t_s, KV_LORA))
            kr_s.append(kr[n_p:, :QK_ROPE].reshape(n_seq, t_s, QK_ROPE))
        x, h2, top_idx, gates = norm(
            x, y, mp, ms, gate_k=2, ln_g=ln_g[l, 0], ln_b=ln_b[l, 0], mod_k=(3, 4), h_dtype=F32,
            router=(moe_w_router[l], moe_b_router[l]), alpha=alpha, name="norm_mix")
        y = _moe(h2, top_idx, gates, moe_w_gu, moe_b_gu, moe_w_down, moe_b_down, l)
        if l + 1 < depth:
            x, h = norm(x, y, mp, ms, gate_k=5, ln_g=ln_g[l, 1], ln_b=ln_b[l, 1], mod_k=(0, 1),
                        h_mods=(mods_p[l + 1], mods_s[l + 1]), alpha=alpha, name="norm_moe")
        else:
            x = norm(x, y, mp, ms, gate_k=5, ln_g=ln_g[l, 1], ln_b=ln_b[l, 1], alpha=alpha,
                     name="norm_last")[0]
    return (x[:n_p].reshape(batch, seq_len, d), x[n_p:].reshape(n_seq, t_s, d),
            jnp.stack(lat_p), jnp.stack(kr_p), jnp.stack(hg_p),
            jnp.stack(lat_s), jnp.stack(kr_s), jnp.stack(hg_s))
```

```python
import functools
import math

import jax
import jax.numpy as jnp
from jax import lax
from jax.experimental import pallas as pl
from jax.experimental.pallas import tpu as pltpu

F32 = jnp.float32
BF16 = jnp.bfloat16
I32 = jnp.int32

LANES = 128
VMEM_LIMIT_BYTES = 56 * 1024 * 1024

HEAD = 128
QK_ROPE = 64
Q_LORA = 512
KV_LORA = 512
TOP_K = 4
SWIGLU_LIMIT = 7.0
SWIGLU_ALPHA = 1.702
NORM_EPS = 1e-5
FORGET_FLOOR = 1e-30
ROPE_THETA = 10000.0
MASK_VALUE = -1e30
HIGHEST = lax.Precision.HIGHEST

LN_TM = 256
MM_TM = 512
HG_TB = 128
HG_SUB = 16
HG_G = 2
HG_SB = 8
FL_TQ = 512
DEC_P = 16
MOE_TM = 256
MOE_RB = 6
MOE_TF = 512
GATHER_T = 256
COMBINE_T = 128


def _cparams(sem):
    return pltpu.CompilerParams(dimension_semantics=sem, vmem_limit_bytes=VMEM_LIMIT_BYTES)


def _fit_tile(n, t, align=LANES):
    t = min(t, n) // align * align
    while n % t:
        t -= align
    return t


def _dot(a, b, dims, precision=None):
    return lax.dot_general(a, b, (dims, ((), ())), precision=precision,
                           preferred_element_type=F32)


def _dot_nn(a, b, precision=None):
    return _dot(a, b, ((1,), (0,)), precision)


def _dot_nt(a, b, precision=None):
    return _dot(a, b, ((1,), (1,)), precision)


def _dot_tn(a, b, precision=None):
    return _dot(a, b, ((0,), (0,)), precision)


def _sigmoid(x):
    return 1.0 / (1.0 + jnp.exp(-x))


def _rope128(x, cos_t, sin_a, sin_b):
    return (x * cos_t + pltpu.roll(x, 96, axis=1) * sin_a + pltpu.roll(x, 32, axis=1) * sin_b)


def _mm_kernel(*refs, has_bias, silu_in, has_rope, m_axis):
    x_ref, w_ref = refs[0], refs[1]
    b_ref = refs[2] if has_bias else None
    o_ref, wbf = refs[-2], refs[-1]

    @pl.when(pl.program_id(m_axis) == 0)
    def _():
        wbf[...] = w_ref[...].astype(BF16)

    x = x_ref[...]
    if silu_in:
        x = x.astype(F32)
        x = x * _sigmoid(x)
    acc = _dot_nn(x.astype(BF16), wbf[...])
    if has_bias:
        acc = acc + b_ref[...]
    if has_rope:
        cos_t, sin_a, sin_b = refs[-5][...], refs[-4][...], refs[-3][...]
        pieces = []
        for c in range(0, acc.shape[1], 2 * HEAD):
            pieces.append(acc[:, c:c + HEAD])
            pieces.append(_rope128(acc[:, c + HEAD:c + 2 * HEAD], cos_t, sin_a, sin_b))
        acc = jnp.concatenate(pieces, axis=1)
    o_ref[...] = acc.astype(o_ref.dtype)


def _mm(x, w, bias=None, *, layer=None, out_dtype=F32, tm=MM_TM, tn=512, silu_in=False,
        rope=None, name="mm"):
    M, K = x.shape
    N = w.shape[-1]
    tm = _fit_tile(M, tm, 8)
    tn = _fit_tile(N, tn)
    assert M % tm == 0 and N % tn == 0 and w.shape[-2] == K
    batched = w.ndim == 3 and layer is None
    has_bias = bias is not None
    if w.ndim == 2:
        w = w[None]
        layer = 0
    if has_bias:
        bias = bias.reshape(w.shape[0], 1, N)
    if batched:
        grid = (w.shape[0], N // tn, M // tm)
        x_spec = pl.BlockSpec((tm, K), lambda l, n, m: (m, 0))
        w_spec = pl.BlockSpec((None, K, tn), lambda l, n, m: (l, 0, n))
        b_spec = pl.BlockSpec((None, 1, tn), lambda l, n, m: (l, 0, n))
        o_spec = pl.BlockSpec((None, tm, tn), lambda l, n, m: (l, m, n))
        out_shape = jax.ShapeDtypeStruct((w.shape[0], M, N), out_dtype)
        sem = ("arbitrary", "arbitrary", "arbitrary")
        m_axis = 2
    else:
        grid = (N // tn, M // tm)
        x_spec = pl.BlockSpec((tm, K), lambda n, m: (m, 0))
        w_spec = pl.BlockSpec((None, K, tn), lambda n, m: (layer, 0, n))
        b_spec = pl.BlockSpec((None, 1, tn), lambda n, m: (layer, 0, n))
        o_spec = pl.BlockSpec((tm, tn), lambda n, m: (m, n))
        out_shape = jax.ShapeDtypeStruct((M, N), out_dtype)
        sem = ("arbitrary", "arbitrary")
        m_axis = 1
    in_specs = [x_spec, w_spec] + ([b_spec] if has_bias else [])
    args = (x, w) + ((bias,) if has_bias else ())
    if rope is not None:
        assert not batched and tn % (2 * HEAD) == 0
        in_specs += [pl.BlockSpec((tm, LANES), lambda n, m: (m, 0))] * 3
        args += tuple(rope)
    return pl.pallas_call(
        functools.partial(_mm_kernel, has_bias=has_bias, silu_in=silu_in,
                          has_rope=rope is not None, m_axis=m_axis),
        grid=grid, in_specs=in_specs, out_specs=o_spec, out_shape=out_shape,
        scratch_shapes=[pltpu.VMEM((K, tn), BF16)],
        compiler_params=_cparams(sem), name=name)(*args)


def _topk_softmax(logits, k):
    tm, n_e = logits.shape
    lane = lax.broadcasted_iota(I32, logits.shape, 1).astype(F32)
    l = logits
    vals, idxs = [], []
    for _ in range(k):
        m = jnp.max(l, axis=-1, keepdims=True)
        idx = jnp.min(jnp.where(l == m, lane, float(n_e)), axis=-1, keepdims=True)
        vals.append(m)
        idxs.append(idx)
        l = jnp.where(lane == idx, -jnp.inf, l)
    es = [jnp.exp(v - vals[0]) for v in vals]
    tot = es[0]
    for e in es[1:]:
        tot = tot + e
    k_lane = lax.broadcasted_iota(I32, (tm, k), 1)
    idx_out = jnp.zeros((tm, k), F32)
    gate_out = jnp.zeros((tm, k), F32)
    for j in range(k):
        idx_out = jnp.where(k_lane == j, idxs[j], idx_out)
        gate_out = jnp.where(k_lane == j, es[j] / tot, gate_out)
    return idx_out.astype(I32), gate_out


def _store_tok_major(ref, val):
    t, d = val.shape
    rows = d // LANES
    for k in range(rows):
        ref[pl.ds(k, t, stride=rows), :] = val[:, k * LANES:(k + 1) * LANES].astype(ref.dtype)


def _load_tok_major(ref, t, d):
    rows = d // LANES
    return jnp.concatenate([ref[pl.ds(k, t, stride=rows), :] for k in range(rows)], axis=1)


def _norm_kernel(*refs, n_p_tiles, alpha, first, has_h, has_router, top_k, tok_major):
    it = iter(refs)
    x_ref = next(it)
    is_p = pl.program_id(0) < n_p_tiles
    x = x_ref[...]
    if first:
        xn = x
    else:
        y_ref, gp, gs, g_ref, b_ref = next(it), next(it), next(it), next(it), next(it)
        gate = jnp.where(is_p, gp[...], gs[...])
        z = alpha * x + (1.0 + gate) * y_ref[...].astype(F32)
        mu = jnp.mean(z, axis=-1, keepdims=True)
        zc = z - mu
        var = jnp.mean(zc * zc, axis=-1, keepdims=True)
        xn = zc * lax.rsqrt(var + NORM_EPS) * g_ref[...] + b_ref[...]
    if has_h:
        sp, ss, hp, hs = next(it), next(it), next(it), next(it)
        scale = jnp.where(is_p, sp[...], ss[...])
        shift = jnp.where(is_p, hp[...], hs[...])
        h = xn * (1.0 + scale) + shift
    if has_router:
        wr, br = next(it), next(it)
    if not first:
        xn_ref = next(it)
        xn_ref[...] = xn
    if has_h:
        h_ref = next(it)
        if tok_major:
            _store_tok_major(h_ref, h)
        else:
            h_ref[...] = h.astype(h_ref.dtype)
    if has_router:
        idx_ref, gate_ref = next(it), next(it)
        logits = _dot_nn(h, wr[...], HIGHEST) + br[...]
        idx, gates = _topk_softmax(logits, top_k)
        idx_ref[...] = idx
        gate_ref[...] = gates


def _norm_mod(x, y, mods_p, mods_s, layer, *, seq_len, n_prompt, gate_k=None, ln_g=None,
              ln_b=None, mod_k=None, h_layer=None, h_dtype=BF16, tok_major=False, router=None,
              alpha=1.0, name="norm"):
    N, D = x.shape
    tm = LN_TM
    assert N % tm == 0 and n_prompt % tm == 0 and seq_len % tm == 0
    n_p_tiles = n_prompt // tm
    n_seq = mods_p.shape[1]
    first = y is None
    has_h = mod_k is not None
    has_router = router is not None
    h_layer = layer if h_layer is None else h_layer

    def p_spec(lyr, k):
        return pl.BlockSpec((None, None, 1, D),
                            lambda i: (lyr, jnp.minimum(i * tm // seq_len, n_seq - 1), 0, k))

    def s_spec(lyr, k):
        return pl.BlockSpec((None, tm, D), lambda i: (lyr, jnp.maximum(i - n_p_tiles, 0), k))

    row_spec = pl.BlockSpec((tm, D), lambda i: (i, 0))
    vec_spec = pl.BlockSpec((1, D), lambda i: (0, 0))
    args, in_specs = [x], [row_spec]
    if not first:
        args += [y, mods_p, mods_s, ln_g.reshape(1, D), ln_b.reshape(1, D)]
        in_specs += [row_spec, p_spec(layer, gate_k), s_spec(layer, gate_k), vec_spec, vec_spec]
    if has_h:
        shift_k, scale_k = mod_k
        args += [mods_p, mods_s, mods_p, mods_s]
        in_specs += [p_spec(h_layer, scale_k), s_spec(h_layer, scale_k),
                     p_spec(h_layer, shift_k), s_spec(h_layer, shift_k)]
    out_shape, out_specs = [], []
    if has_router:
        w_r, b_r = router
        n_e = w_r.shape[-1]
        args += [w_r, b_r.reshape(1, n_e)]
        in_specs += [pl.BlockSpec((D, n_e), lambda i: (0, 0)),
                     pl.BlockSpec((1, n_e), lambda i: (0, 0))]
    if not first:
        out_shape.append(jax.ShapeDtypeStruct((N, D), F32))
        out_specs.append(row_spec)
    if has_h and tok_major:
        rows = D // LANES
        out_shape.append(jax.ShapeDtypeStruct((N * rows, LANES), h_dtype))
        out_specs.append(pl.BlockSpec((tm * rows, LANES), lambda i: (i, 0)))
    elif has_h:
        out_shape.append(jax.ShapeDtypeStruct((N, D), h_dtype))
        out_specs.append(row_spec)
    if has_router:
        out_shape += [jax.ShapeDtypeStruct((N, TOP_K), I32), jax.ShapeDtypeStruct((N, TOP_K), F32)]
        out_specs += [pl.BlockSpec((tm, TOP_K), lambda i: (i, 0))] * 2
    return pl.pallas_call(
        functools.partial(_norm_kernel, n_p_tiles=n_p_tiles, alpha=alpha, first=first,
                          has_h=has_h, has_router=has_router, top_k=TOP_K, tok_major=tok_major),
        grid=(N // tm,), in_specs=in_specs, out_specs=out_specs, out_shape=out_shape,
        compiler_params=_cparams(("arbitrary",)), name=name)(*args)


def _hgrn_unit(qr, fr, ir, gr, lb, ng, s_prev, sub, kpad, bpad, vpad):
    tb = qr.shape[0]
    ns = tb // sub
    q = qr * _sigmoid(qr)
    forget = lb + (1.0 - lb) * _sigmoid(fr)
    lf = jnp.log(jnp.maximum(forget, FORGET_FLOOR))
    kin = (1.0 - lb) * _sigmoid(-fr)
    v = ir

    row = lax.broadcasted_iota(I32, (tb, tb), 0)
    col = lax.broadcasted_iota(I32, (tb, tb), 1)
    tri = jnp.where(col <= row, 1.0, 0.0).astype(F32)
    cum = _dot_nn(tri, lf, HIGHEST)
    if ns > 1:
        prev = jnp.where(col < (row // sub) * sub, 1.0, 0.0).astype(F32)
        ref = _dot_nn(prev, lf, HIGHEST)
        loc = cum - ref
    else:
        ref = None
        loc = cum
    ones = jnp.ones((tb, HEAD), F32)
    tot_col = _dot_tn(lf, ones, HIGHEST)

    o = _dot_nn((q * jnp.exp(cum)).astype(BF16), s_prev.astype(BF16))

    kpad[pl.ds(sub, tb), :] = kin
    bpad[pl.ds(sub, tb), :] = loc
    vpad[pl.ds(sub, tb), :] = v
    pos = lax.broadcasted_iota(I32, (tb, 1), 0) % sub
    for d in range(sub):
        k_sh = kpad[pl.ds(sub - d, tb), :]
        b_sh = bpad[pl.ds(sub - d, tb), :]
        v_sh = vpad[pl.ds(sub - d, tb), :]
        p = q * k_sh * jnp.exp(jnp.minimum(loc - b_sh, 0.0))
        a = jnp.sum(p, axis=-1, keepdims=True)
        o = o + jnp.where(pos >= d, a, 0.0) * v_sh

    if ns > 1:
        parts = [o[:sub]]
        for i in range(1, ns):
            lo = i * sub
            q_t = (q[lo:lo + sub] * jnp.exp(loc[lo:lo + sub])).astype(BF16)
            k_t = (kin[:lo] * jnp.exp(jnp.minimum(ref[lo:lo + 1] - cum[:lo], 0.0))).astype(BF16)
            att = _dot_nt(q_t, k_t)
            parts.append(o[lo:lo + sub] + _dot_nn(att.astype(BF16), v[:lo].astype(BF16)))
        o = jnp.concatenate(parts, axis=0)

    k_e = (kin * jnp.exp(cum[tb - 1:tb] - cum)).astype(BF16)
    s_new = jnp.exp(tot_col) * s_prev + _dot_tn(k_e, v.astype(BF16))

    o = o * lax.rsqrt(jnp.mean(o * o, axis=-1, keepdims=True) + NORM_EPS) * ng
    o = o * _sigmoid(gr)
    return o, s_new


def _hgrn_prompt_kernel(q_ref, f_ref, i_ref, g_ref, lb_ref, ng_ref, o_ref, s_out, state,
                        kpad, bpad, vpad, *, n_heads, sub):
    c = pl.program_id(2)

    @pl.when(c == 0)
    def _():
        state[...] = jnp.zeros_like(state)
        kpad[...] = jnp.zeros_like(kpad)
        bpad[...] = jnp.zeros_like(bpad)
        vpad[...] = jnp.zeros_like(vpad)

    for h in range(n_heads):
        sl = slice(h * HEAD, (h + 1) * HEAD)
        o, s_new = _hgrn_unit(q_ref[:, sl], f_ref[:, sl], i_ref[:, sl], g_ref[:, sl],
                              lb_ref[:, sl], ng_ref[...], state[h], sub, kpad, bpad, vpad)
        o_ref[:, sl] = o.astype(o_ref.dtype)
        state[h] = s_new

    @pl.when(c == pl.num_programs(2) - 1)
    def _():
        s_out[...] = state[...]


def _hgrn_prompt(proj, lb, ng, *, batch, seq_len, row0=0):
    hf = proj.shape[1] // 4
    n_h = hf // HEAD
    g, tb, sub = HG_G, HG_TB, HG_SUB
    assert n_h % g == 0 and seq_len % tb == 0 and row0 % tb == 0
    nc = seq_len // tb
    ncol = hf // (g * HEAD)
    r0 = row0 // tb

    def seg_spec(seg):
        return pl.BlockSpec((tb, g * HEAD), lambda b, hg, c: (r0 + b * nc + c, seg * ncol + hg))

    return pl.pallas_call(
        functools.partial(_hgrn_prompt_kernel, n_heads=g, sub=sub),
        grid=(batch, n_h // g, nc),
        in_specs=[seg_spec(0), seg_spec(1), seg_spec(2), seg_spec(3),
                  pl.BlockSpec((1, g * HEAD), lambda b, hg, c: (0, hg)),
                  pl.BlockSpec((1, HEAD), lambda b, hg, c: (0, 0))],
        out_specs=[pl.BlockSpec((tb, g * HEAD), lambda b, hg, c: (b * nc + c, hg)),
                   pl.BlockSpec((None, g, HEAD, HEAD), lambda b, hg, c: (b, hg, 0, 0))],
        out_shape=[jax.ShapeDtypeStruct((batch * seq_len, hf), BF16),
                   jax.ShapeDtypeStruct((batch, n_h, HEAD, HEAD), F32)],
        scratch_shapes=[pltpu.VMEM((g, HEAD, HEAD), F32)]
        + [pltpu.VMEM((sub + tb, HEAD), F32)] * 3,
        compiler_params=_cparams(("arbitrary", "arbitrary", "arbitrary")),
        name="hgrn_prompt")(proj, proj, proj, proj, lb.reshape(1, hf), ng.reshape(1, HEAD))


def _hgrn_sample_kernel(q_ref, f_ref, i_ref, g_ref, lb_ref, ng_ref, s_ref, o_ref, s_out,
                        kpad, bpad, vpad, *, n_seq, n_heads, t):
    kpad[...] = jnp.zeros_like(kpad)
    bpad[...] = jnp.zeros_like(bpad)
    vpad[...] = jnp.zeros_like(vpad)
    for s in range(n_seq):
        rows = slice(s * t, (s + 1) * t)
        for h in range(n_heads):
            sl = slice(h * HEAD, (h + 1) * HEAD)
            o, s_new = _hgrn_unit(q_ref[rows, sl], f_ref[rows, sl], i_ref[rows, sl],
                                  g_ref[rows, sl], lb_ref[:, sl], ng_ref[...], s_ref[s, h],
                                  t, kpad, bpad, vpad)
            o_ref[rows, sl] = o.astype(o_ref.dtype)
            s_out[s, h] = s_new


def _hgrn_sample(proj, lb, ng, s0, layer, *, t, row0):
    hf = proj.shape[1] // 4
    n_h = hf // HEAD
    n_seq = s0.shape[1]
    sb, g = HG_SB, HG_G
    rb = sb * t
    assert n_seq % sb == 0 and n_h % g == 0 and row0 % rb == 0
    ncol = hf // (g * HEAD)
    r0 = row0 // rb

    def seg_spec(seg):
        return pl.BlockSpec((rb, g * HEAD), lambda s, hg: (r0 + s, seg * ncol + hg))

    return pl.pallas_call(
        functools.partial(_hgrn_sample_kernel, n_seq=sb, n_heads=g, t=t),
        grid=(n_seq // sb, n_h // g),
        in_specs=[seg_spec(0), seg_spec(1), seg_spec(2), seg_spec(3),
                  pl.BlockSpec((1, g * HEAD), lambda s, hg: (0, hg)),
                  pl.BlockSpec((1, HEAD), lambda s, hg: (0, 0)),
                  pl.BlockSpec((None, sb, g, HEAD, HEAD), lambda s, hg: (layer, s, hg, 0, 0))],
        out_specs=[pl.BlockSpec((rb, g * HEAD), lambda s, hg: (s, hg)),
                   pl.BlockSpec((sb, g, HEAD, HEAD), lambda s, hg: (s, hg, 0, 0))],
        out_shape=[jax.ShapeDtypeStruct((n_seq * t, hf), BF16),
                   jax.ShapeDtypeStruct(s0.shape[1:], F32)],
        scratch_shapes=[pltpu.VMEM((2 * t, HEAD), F32)] * 3,
        compiler_params=_cparams(("arbitrary", "arbitrary")),
        name="hgrn_sample")(proj, proj, proj, proj, lb.reshape(1, hf), ng.reshape(1, HEAD), s0)


def _mla_post_kernel(a_ref, qg_ref, kg_ref, cos_ref, sa_ref, sb_ref,
                     cq_ref, ckv_ref, kr_ref, krb_ref):
    a = a_ref[...]
    cq = a[:, :Q_LORA]
    cq = cq * lax.rsqrt(jnp.mean(cq * cq, axis=-1, keepdims=True) + NORM_EPS) * qg_ref[...]
    cq_ref[...] = cq.astype(cq_ref.dtype)
    ckv = a[:, Q_LORA:Q_LORA + KV_LORA]
    ckv = ckv * lax.rsqrt(jnp.mean(ckv * ckv, axis=-1, keepdims=True) + NORM_EPS) * kg_ref[...]
    ckv_ref[...] = ckv
    kr = _rope128(a[:, Q_LORA + KV_LORA:], cos_ref[...], sa_ref[...], sb_ref[...])
    kr_ref[...] = kr
    krb_ref[...] = kr.astype(krb_ref.dtype)


def _mla_post(a, q_g, kv_g, cos_t, sin_a, sin_b):
    n = a.shape[0]
    tm = LN_TM
    row = lambda w: pl.BlockSpec((tm, w), lambda i: (i, 0))
    vec = lambda w: pl.BlockSpec((1, w), lambda i: (0, 0))
    return pl.pallas_call(
        _mla_post_kernel, grid=(n // tm,),
        in_specs=[row(a.shape[1]), vec(Q_LORA), vec(KV_LORA), row(LANES), row(LANES), row(LANES)],
        out_specs=[row(Q_LORA), row(KV_LORA), row(LANES), row(LANES)],
        out_shape=[jax.ShapeDtypeStruct((n, Q_LORA), BF16),
                   jax.ShapeDtypeStruct((n, KV_LORA), F32),
                   jax.ShapeDtypeStruct((n, LANES), F32),
                   jax.ShapeDtypeStruct((n, LANES), BF16)],
        compiler_params=_cparams(("arbitrary",)), name="mla_post")(
            a, q_g.reshape(1, Q_LORA), kv_g.reshape(1, KV_LORA), cos_t, sin_a, sin_b)


def _flash_kernel(q_ref, kv_ref, kr_ref, o_ref, *, scale, t):
    qi = pl.program_id(2)
    q = q_ref[...]

    def block(kc, carry, diagonal):
        m_prev, l_prev, acc = carry
        rows = pl.ds(pl.multiple_of(kc * t, t), t)
        k = jnp.concatenate([kv_ref[rows, :HEAD], kr_ref[rows, :]], axis=1)
        s = _dot_nt(q, k) * scale
        if diagonal:
            qpos = lax.broadcasted_iota(I32, (t, t), 0)
            kpos = lax.broadcasted_iota(I32, (t, t), 1)
            s = jnp.where(kpos <= qpos, s, MASK_VALUE)
        m_new = jnp.maximum(m_prev, jnp.max(s, axis=-1, keepdims=True))
        alpha = jnp.exp(m_prev - m_new)
        p = jnp.exp(s - m_new)
        l_new = alpha * l_prev + jnp.sum(p, axis=-1, keepdims=True)
        acc = alpha * acc + _dot_nn(p.astype(BF16), kv_ref[rows, HEAD:])
        return m_new, l_new, acc

    init = (jnp.full((t, 1), MASK_VALUE, F32), jnp.zeros((t, 1), F32), jnp.zeros((t, HEAD), F32))
    carry = lax.fori_loop(0, qi, lambda kc, c: block(kc, c, False), init)
    _, l_fin, acc = block(qi, carry, True)
    o_ref[...] = (acc / l_fin).astype(o_ref.dtype)


def _flash(q, kv, krb, *, batch, seq_len, scale):
    n_h = q.shape[1] // (2 * HEAD)
    t = min(FL_TQ, seq_len)
    nq = seq_len // t
    return pl.pallas_call(
        functools.partial(_flash_kernel, scale=scale, t=t),
        grid=(batch, n_h, nq),
        in_specs=[pl.BlockSpec((t, 2 * HEAD), lambda b, h, qi: (b * nq + qi, h)),
                  pl.BlockSpec((seq_len, 2 * HEAD), lambda b, h, qi: (b, h)),
                  pl.BlockSpec((seq_len, LANES), lambda b, h, qi: (b, 0))],
        out_specs=pl.BlockSpec((t, HEAD), lambda b, h, qi: (b * nq + qi, h)),
        out_shape=jax.ShapeDtypeStruct((batch * seq_len, n_h * HEAD), BF16),
        compiler_params=_cparams(("arbitrary",) * 3), name="mla_flash")(q, kv, krb)


def _q_absorb_kernel(q_ref, w_ref, ql_ref, qr_ref, *, t):
    x = q_ref[...]
    n = x.shape[0]
    ql = _dot_nt(x[:, :HEAD], w_ref[...].astype(BF16))
    ql_ref[...] = ql.reshape(n // t, t, KV_LORA)
    qr_ref[...] = x[:, HEAD:].astype(F32).reshape(n // t, t, HEAD)


def _q_absorb(q_s, w_kv_up, layer, *, t):
    ns, w = q_s.shape
    n_h = w // (2 * HEAD)
    n_seq = ns // t
    return pl.pallas_call(
        functools.partial(_q_absorb_kernel, t=t), grid=(n_h,),
        in_specs=[pl.BlockSpec((ns, 2 * HEAD), lambda h: (0, h)),
                  pl.BlockSpec((None, KV_LORA, HEAD), lambda h: (layer, 0, 2 * h))],
        out_specs=[pl.BlockSpec((n_seq, None, t, KV_LORA), lambda h: (0, h, 0, 0)),
                   pl.BlockSpec((n_seq, None, t, HEAD), lambda h: (0, h, 0, 0))],
        out_shape=[jax.ShapeDtypeStruct((n_seq, n_h, t, KV_LORA), F32),
                   jax.ShapeDtypeStruct((n_seq, n_h, t, HEAD), F32)],
        compiler_params=_cparams(("arbitrary",)), name="q_absorb")(q_s, w_kv_up)


def _decode_kernel(pt_ref, ql_ref, qr_ref, cn_ref, kn_ref, lat_hbm, kr_hbm, o_ref,
                   lat_buf, kr_buf, sem, m_sc, l_sc, acc_sc, *, layer, scale, n_pg, t):
    j = pl.program_id(1)
    nj = pl.num_programs(1)
    step = pl.program_id(0) * nj + j
    n_steps = pl.num_programs(0) * nj
    slot = step % 2

    def page_copies(stp, slt):
        out = []
        for i in range(n_pg):
            pg = pt_ref[stp * n_pg + i]
            out.append(pltpu.make_async_copy(lat_hbm.at[layer, pg], lat_buf.at[slt, i],
                                             sem.at[0, slt]))
            out.append(pltpu.make_async_copy(kr_hbm.at[layer, pg], kr_buf.at[slt, i],
                                             sem.at[1, slt]))
        return out

    @pl.when(step == 0)
    def _():
        for c in page_copies(0, 0):
            c.start()

    @pl.when(step + 1 < n_steps)
    def _():
        for c in page_copies(step + 1, 1 - slot):
            c.start()

    for c in page_copies(step, slot):
        c.wait()

    @pl.when(j == 0)
    def _():
        m_sc[...] = jnp.full_like(m_sc, MASK_VALUE)
        l_sc[...] = jnp.zeros_like(l_sc)
        acc_sc[...] = jnp.zeros_like(acc_sc)

    q = ql_ref[...].astype(BF16)
    qr = qr_ref[...][:, :QK_ROPE].astype(BF16)

    def update(s, values):
        m_prev = m_sc[...]
        m_new = jnp.maximum(m_prev, jnp.max(s, axis=-1, keepdims=True))
        alpha = jnp.exp(m_prev - m_new)
        p = jnp.exp(s - m_new)
        l_sc[...] = alpha * l_sc[...] + jnp.sum(p, axis=-1, keepdims=True)
        acc = alpha * acc_sc[...]
        w = 0
        for val in values:
            n = val.shape[0]
            acc = acc + _dot_nn(p[:, w:w + n].astype(BF16), val)
            w += n
        acc_sc[...] = acc
        m_sc[...] = m_new

    lats = [lat_buf[slot, i].astype(BF16) for i in range(n_pg)]
    s = jnp.concatenate(
        [_dot_nt(q, lats[i]) + _dot_nn(qr, kr_buf[slot, i].astype(BF16)) for i in range(n_pg)],
        axis=1) * scale
    update(s, lats)

    @pl.when(j == nj - 1)
    def _():
        cn = cn_ref[...].astype(BF16)
        kn = kn_ref[...][:, :QK_ROPE].astype(BF16)
        s_new = (_dot_nt(q, cn) + _dot_nt(qr, kn)) * scale
        rows = s_new.shape[0]
        qpos = lax.broadcasted_iota(I32, (rows, t), 0) % t
        kpos = lax.broadcasted_iota(I32, (rows, t), 1)
        update(jnp.where(kpos <= qpos, s_new, MASK_VALUE), [cn])
        o_ref[...] = acc_sc[...] / l_sc[...]


def _decode(q_lat, q_rp, ckv_new, kr_new, cache_lat, cache_kr, page_table, layer, *, scale):
    n_seq, rows, _ = q_lat.shape
    t = ckv_new.shape[1]
    n_pages = page_table.shape[1]
    page = cache_lat.shape[2]
    n_pg = min(DEC_P, n_pages)
    assert n_pages % n_pg == 0

    seq3 = lambda r, w: pl.BlockSpec((None, r, w), lambda b, j, pt: (b, 0, 0))
    hbm = pl.BlockSpec(memory_space=pl.ANY)
    grid_spec = pltpu.PrefetchScalarGridSpec(
        num_scalar_prefetch=1, grid=(n_seq, n_pages // n_pg),
        in_specs=[seq3(rows, KV_LORA), seq3(rows, HEAD), seq3(t, KV_LORA), seq3(t, HEAD),
                  hbm, hbm],
        out_specs=seq3(rows, KV_LORA),
        scratch_shapes=[pltpu.VMEM((2, n_pg, page, KV_LORA), F32),
                        pltpu.VMEM((2, n_pg, QK_ROPE, page), F32),
                        pltpu.SemaphoreType.DMA((2, 2)),
                        pltpu.VMEM((rows, 1), F32), pltpu.VMEM((rows, 1), F32),
                        pltpu.VMEM((rows, KV_LORA), F32)])
    return pl.pallas_call(
        functools.partial(_decode_kernel, layer=layer, scale=scale, n_pg=n_pg, t=t),
        grid_spec=grid_spec,
        out_shape=jax.ShapeDtypeStruct((n_seq, rows, KV_LORA), F32),
        compiler_params=_cparams(("arbitrary", "arbitrary")), name="mla_decode")(
            page_table.reshape(-1), q_lat, q_rp, ckv_new, kr_new, cache_lat, cache_kr)


def _o_up_kernel(x_ref, w_ref, o_ref):
    x = x_ref[...]
    n_seq, t, c = x.shape
    o = _dot_nn(x.reshape(n_seq * t, c).astype(BF16), w_ref[...].astype(BF16))
    o_ref[...] = o.astype(o_ref.dtype)


def _o_up(o_lat, w_kv_up, layer):
    n_seq, n_h, t, c = o_lat.shape
    return pl.pallas_call(
        _o_up_kernel, grid=(n_h,),
        in_specs=[pl.BlockSpec((n_seq, None, t, c), lambda h: (0, h, 0, 0)),
                  pl.BlockSpec((None, c, HEAD), lambda h: (layer, 0, 2 * h + 1))],
        out_specs=pl.BlockSpec((n_seq * t, HEAD), lambda h: (0, h)),
        out_shape=jax.ShapeDtypeStruct((n_seq * t, n_h * HEAD), BF16),
        compiler_params=_cparams(("arbitrary",)), name="o_up")(o_lat, w_kv_up)


def _gather_kernel(idx_ref, src_ref, o_ref, buf, sem, *, n_tok, rows, d):
    def tok_copy(r):
        src = src_ref.at[pl.ds(pl.multiple_of(idx_ref[0, r], rows), rows)]
        dst = buf.at[pl.ds(pl.multiple_of(r * rows, rows), rows)]
        return pltpu.make_async_copy(src, dst, sem)

    def start(r, c):
        tok_copy(r).start()
        return c

    def wait(r, c):
        tok_copy(r).wait()
        return c

    lax.fori_loop(0, n_tok, start, 0, unroll=8)
    lax.fori_loop(0, n_tok, wait, 0, unroll=8)
    o_ref[...] = _load_tok_major(buf, n_tok, d).astype(o_ref.dtype)


def _gather_rows(src, idx, d, *, out_dtype=BF16):
    n = idx.shape[0]
    rows = d // LANES
    t = GATHER_T
    assert n % t == 0
    return pl.pallas_call(
        functools.partial(_gather_kernel, n_tok=t, rows=rows, d=d), grid=(n // t,),
        in_specs=[pl.BlockSpec((None, 1, t), lambda i: (i, 0, 0), memory_space=pltpu.SMEM),
                  pl.BlockSpec(memory_space=pl.ANY)],
        out_specs=pl.BlockSpec((t, d), lambda i: (i, 0)),
        out_shape=jax.ShapeDtypeStruct((n, d), out_dtype),
        scratch_shapes=[pltpu.VMEM((t * rows, LANES), F32), pltpu.SemaphoreType.DMA(())],
        compiler_params=_cparams(("arbitrary",)), name="moe_gather")(idx.reshape(n // t, 1, t), src)


def _combine_kernel(idx_ref, gate_ref, src_ref, o_ref, buf, g_sc, acc_sc, sem, *,
                    n_tok, k, rows, d):
    def tok_copy(tok, j):
        src = src_ref.at[pl.ds(pl.multiple_of(idx_ref[0, tok * k + j], rows), rows)]
        dst = buf.at[j, pl.ds(pl.multiple_of(tok * rows, rows), rows)]
        return pltpu.make_async_copy(src, dst, sem)

    def start(tok, c):
        for j in range(k):
            tok_copy(tok, j).start()
        return c

    def wait(tok, c):
        for j in range(k):
            tok_copy(tok, j).wait()
        return c

    lax.fori_loop(0, n_tok, start, 0, unroll=2)
    g = gate_ref[...]
    for i in range(rows):
        g_sc[pl.ds(i, n_tok, stride=rows), :] = g
    lax.fori_loop(0, n_tok, wait, 0, unroll=2)
    gr = g_sc[...]
    acc = gr[:, 0:1] * buf[0]
    for j in range(1, k):
        acc = acc + gr[:, j:j + 1] * buf[j]
    acc_sc[...] = acc
    o_ref[...] = _load_tok_major(acc_sc, n_tok, d)


def _combine(yb, dest, gates, d):
    n, k = dest.shape
    rows = d // LANES
    t = COMBINE_T
    assert n % t == 0
    return pl.pallas_call(
        functools.partial(_combine_kernel, n_tok=t, k=k, rows=rows, d=d), grid=(n // t,),
        in_specs=[pl.BlockSpec((None, 1, t * k), lambda i: (i, 0, 0), memory_space=pltpu.SMEM),
                  pl.BlockSpec((t, k), lambda i: (i, 0)),
                  pl.BlockSpec(memory_space=pl.ANY)],
        out_specs=pl.BlockSpec((t, d), lambda i: (i, 0)),
        out_shape=jax.ShapeDtypeStruct((n, d), F32),
        scratch_shapes=[pltpu.VMEM((k, t * rows, LANES), F32), pltpu.VMEM((t * rows, k), F32),
                        pltpu.VMEM((t * rows, LANES), F32), pltpu.SemaphoreType.DMA(())],
        compiler_params=_cparams(("arbitrary",)), name="moe_combine")(
            dest.reshape(n // t, 1, t * k), gates, yb)


def _moe_ffn_kernel(se_ref, sb_ref, sn_ref, sz_ref, x_ref, wg_ref, wu_ref, wd_ref, bg_ref,
                    bu_ref, bd_ref, o_ref, wg_bf, wu_bf, wd_bf, acc):
    s, j, r = pl.program_id(0), pl.program_id(1), pl.program_id(2)
    nj = pl.num_programs(1)
    n_sub = sn_ref[s]

    @pl.when(jnp.logical_and(j == nj - 1, r < sz_ref[s]))
    def _():
        o_ref[...] = jnp.zeros_like(o_ref)

    @pl.when(jnp.logical_and(r == 0, n_sub > 0))
    def _():
        wg_bf[...] = wg_ref[...].astype(BF16)
        wu_bf[...] = wu_ref[...].astype(BF16)
        wd_bf[...] = wd_ref[...].astype(BF16)

    @pl.when(r < n_sub)
    def _():
        x = x_ref[...]
        gate = jnp.minimum(_dot_nn(x, wg_bf[...]) + bg_ref[...], SWIGLU_LIMIT)
        up = jnp.clip(_dot_nn(x, wu_bf[...]) + bu_ref[...], -SWIGLU_LIMIT, SWIGLU_LIMIT)
        act = (up + 1.0) * (gate * _sigmoid(SWIGLU_ALPHA * gate))
        part = _dot_nn(act.astype(BF16), wd_bf[...])

        @pl.when(j == 0)
        def _():
            acc[r] = part

        @pl.when(jnp.logical_and(j > 0, j < nj - 1))
        def _():
            acc[r] = acc[r] + part

        @pl.when(j == nj - 1)
        def _():
            _store_tok_major(o_ref, acc[r] + part + bd_ref[...])


def _moe_ffn(xb, w_gu, b_gu, w_down, b_down, layer, sb_expert, sb_block, sb_nsub, sb_nzero):
    d = xb.shape[1]
    tm, rb, tf = MOE_TM, MOE_RB, MOE_TF
    rows = d // LANES
    n_blocks = xb.shape[0] // tm
    d_ff = w_down.shape[2]
    n_e = w_down.shape[1]
    nj = d_ff // tf
    n_super = sb_expert.shape[0]
    assert nj >= 2 and rb >= 4

    def jeff(s, j, sn):
        return jnp.where(sn[s] > 0, j, nj - 1)

    def x_map(s, j, r, se, sb, sn, sz):
        return (sb[s] + jnp.minimum(r, jnp.maximum(sn[s] - 1, 0)), 0)

    def o_map(s, j, r, se, sb, sn, sz):
        cnt = jnp.where(sn[s] > 0, sn[s], sz[s])
        r_last = jnp.minimum(r, jnp.maximum(cnt - 1, 0))
        return (sb[s] + jnp.where(j == nj - 1, r_last, 0), 0)

    def w_spec(shape, fn, switch_r=None):
        def index_map(s, j, r, se, sb, sn, sz):
            if switch_r is None:
                return fn(se[s], jeff(s, j, sn))
            wrap = j == nj - 1
            s2 = jnp.where(wrap, jnp.minimum(s + 1, n_super - 1), s)
            j2 = jnp.where(wrap, jnp.where(s == n_super - 1, nj - 1, 0), j + 1)
            ahead = r >= switch_r
            s_use = jnp.where(ahead, s2, s)
            return fn(se[s_use], jeff(s_use, jnp.where(ahead, j2, j), sn))
        return pl.BlockSpec(shape, index_map)

    grid_spec = pltpu.PrefetchScalarGridSpec(
        num_scalar_prefetch=4, grid=(n_super, nj, rb),
        in_specs=[
            pl.BlockSpec((tm, d), x_map),
            w_spec((None, None, d, tf), lambda e, jj: (layer, e, 0, jj), 1),
            w_spec((None, None, d, tf), lambda e, jj: (layer, e, 0, nj + jj), 2),
            w_spec((None, None, tf, d), lambda e, jj: (layer, e, jj, 0), 3),
            w_spec((None, None, 1, tf), lambda e, jj: (layer, e, 0, jj)),
            w_spec((None, None, 1, tf), lambda e, jj: (layer, e, 0, nj + jj)),
            w_spec((None, None, 1, d), lambda e, jj: (layer, e, 0, 0)),
        ],
        out_specs=pl.BlockSpec((tm * rows, LANES), o_map),
        scratch_shapes=[pltpu.VMEM((d, tf), BF16), pltpu.VMEM((d, tf), BF16),
                        pltpu.VMEM((tf, d), BF16), pltpu.VMEM((rb, tm, d), F32)])
    n_l = w_gu.shape[0]
    return pl.pallas_call(
        _moe_ffn_kernel, grid_spec=grid_spec,
        out_shape=jax.ShapeDtypeStruct((n_blocks * tm * rows, LANES), F32),
        compiler_params=_cparams(("arbitrary", "arbitrary", "arbitrary")), name="moe_ffn")(
            sb_expert, sb_block, sb_nsub, sb_nzero, xb, w_gu, w_gu, w_down,
            b_gu.reshape(n_l, n_e, 1, 2 * d_ff), b_gu.reshape(n_l, n_e, 1, 2 * d_ff),
            b_down.reshape(n_l, n_e, 1, d))


def _route(top_idx, n_e):
    n, k = top_idx.shape
    a = n * k
    tm, rb = MOE_TM, MOE_RB
    flat_e = top_idx.reshape(a)
    onehot = (flat_e[:, None] == jnp.arange(n_e, dtype=I32)[None, :]).astype(I32)
    csum = jnp.cumsum(onehot, axis=0)
    rank = jnp.sum((csum - 1) * onehot, axis=1)
    counts = csum[-1]
    nblk = (counts + tm - 1) // tm
    blk_end = jnp.cumsum(nblk)
    blk_start = blk_end - nblk
    dest = blk_start[flat_e] * tm + rank
    n_blocks = a // tm + n_e
    row_tok = jnp.zeros((n_blocks * tm,), I32).at[dest].set(jnp.arange(a, dtype=I32) // k)
    n_super = -(-n_blocks // rb) + n_e
    nsup = (nblk + rb - 1) // rb
    sup_end = jnp.cumsum(nsup)
    sup_start = sup_end - nsup
    sid = jnp.arange(n_super, dtype=I32)
    e_of = jnp.minimum(jnp.searchsorted(sup_end, sid, side="right"), n_e - 1).astype(I32)
    used = sid < sup_end[-1]
    local = sid - sup_start[e_of]
    sb_block = blk_start[e_of] + local * rb
    sb_nsub = jnp.clip(nblk[e_of] - local * rb, 0, rb)
    last = jnp.maximum(sup_end[-1] - 1, 0)
    sb_expert = jnp.where(used, e_of, e_of[last]).astype(I32)
    z_start = blk_end[-1] + (sid - sup_end[-1]) * rb
    sb_nzero = jnp.where(used, 0, jnp.clip(n_blocks - z_start, 0, rb)).astype(I32)
    sb_block = jnp.where(used, sb_block, jnp.minimum(z_start, n_blocks - 1)).astype(I32)
    sb_nsub = jnp.where(used, sb_nsub, 0).astype(I32)
    return row_tok, dest.reshape(n, k).astype(I32), sb_expert, sb_block, sb_nsub, sb_nzero


def _moe(h2, top_idx, gates, w_gu, b_gu, w_down, b_down, layer):
    n_e, _, d = w_down.shape[1:]
    rows = d // LANES
    row_tok, dest, sb_e, sb_b, sb_n, sb_z = _route(top_idx, n_e)
    xb = _gather_rows(h2, row_tok * rows, d)
    yb = _moe_ffn(xb, w_gu, b_gu, w_down, b_down, layer, sb_e, sb_b, sb_n, sb_z)
    return _combine(yb, dest * rows, gates, d)


def _rope_tables(pos):
    half = QK_ROPE // 2
    inv = ROPE_THETA ** (-jnp.arange(half, dtype=F32) / half)
    ang = pos.astype(F32)[:, None] * inv[None, :]
    cos, sin = jnp.cos(ang), jnp.sin(ang)
    z = jnp.zeros_like(cos)
    cos_t = jnp.concatenate([cos, cos, z, z], axis=1)
    sin_a = jnp.concatenate([-sin, z, z, z], axis=1)
    sin_b = jnp.concatenate([z, sin, z, z], axis=1)
    return cos_t, sin_a, sin_b


def kernel(x_prompt, x_sample, c_prompt, c_sample, cache_mla_latent, cache_mla_krope, state_hgrn, page_table, ada_w, ada_b, ln_g, ln_b, hg_lb_logits, hg_w_in, hg_norm_g, hg_w_out, mla_w_in, mla_q_norm_g, mla_kv_norm_g, mla_w_q_up, mla_w_kv_up, mla_w_out, moe_w_router, moe_b_router, moe_w_gu, moe_b_gu, moe_w_down, moe_b_down):
    batch, seq_len, d = x_prompt.shape
    n_seq, t_s, _ = x_sample.shape
    depth = ada_w.shape[0]
    n_p, n_s = batch * seq_len, n_seq * t_s
    alpha = (2 * depth) ** 0.25
    past = page_table.shape[1] * cache_mla_latent.shape[2]
    n_h = mla_w_out.shape[1] // HEAD
    qk_dim = HEAD + QK_ROPE
    attn_scale = qk_dim ** -0.5

    x = jnp.concatenate([x_prompt.reshape(n_p, d), x_sample.reshape(n_s, d)], axis=0)

    n_c = batch + n_seq
    c_rows = -(-n_c // 8) * 8
    c_all = jnp.concatenate([c_prompt, c_sample, jnp.zeros((c_rows - n_c, d), F32)], axis=0)
    mods = _mm(c_all, ada_w, ada_b, out_dtype=F32, tm=c_rows, tn=1024, silu_in=True, name="adaln")
    mods_p = mods[:, :batch].reshape(depth, batch, 1, 6 * d)
    mods_s = jnp.repeat(mods[:, batch:n_c], t_s, axis=1)

    lb_p = jax.nn.softmax(hg_lb_logits.astype(F32), axis=0)
    lower_bounds = jnp.cumsum(lb_p, axis=0) - lb_p[0]

    pos = jnp.concatenate([jnp.tile(jnp.arange(seq_len), batch), jnp.tile(past + jnp.arange(t_s), n_seq)])
    cos_t, sin_a, sin_b = _rope_tables(pos)

    norm = functools.partial(_norm_mod, seq_len=seq_len, n_prompt=n_p)
    h = norm(x, None, mods_p, mods_s, 0, mod_k=(0, 1), name="modulate0")[0]
    cache_kr_t = jnp.swapaxes(cache_mla_krope, 2, 3)

    lat_p, kr_p, hg_p, lat_s, kr_s, hg_s = [], [], [], [], [], []
    for l in range(depth):
        j = l // 2
        if l % 2 == 0:
            proj = _mm(h, hg_w_in, layer=j, out_dtype=F32, tn=1024, name="hgrn_in")
            o_p, sp = _hgrn_prompt(proj, lower_bounds[j], hg_norm_g[j], batch=batch, seq_len=seq_len)
            o_s, ss = _hgrn_sample(proj, lower_bounds[j], hg_norm_g[j], state_hgrn, j, t=t_s, row0=n_p)
            hg_p.append(sp)
            hg_s.append(ss)
            y = _mm(jnp.concatenate([o_p, o_s], axis=0), hg_w_out, layer=j, out_dtype=F32,
                    tn=1024, name="hgrn_out")
        else:
            w_in = jnp.pad(mla_w_in[j], ((0, 0), (0, LANES - QK_ROPE)))
            a = _mm(h, w_in, out_dtype=F32, tn=w_in.shape[1], name="mla_in")
            cq, ckv, kr, krb = _mla_post(a, mla_q_norm_g[j], mla_kv_norm_g[j], cos_t, sin_a, sin_b)
            w_q = jnp.pad(mla_w_q_up[j].reshape(Q_LORA, n_h, qk_dim),
                          ((0, 0), (0, 0), (0, 2 * HEAD - qk_dim))).reshape(Q_LORA, n_h * 2 * HEAD)
            q = _mm(cq, w_q, out_dtype=BF16, tn=1024, rope=(cos_t, sin_a, sin_b), name="mla_q_up")
            kv = _mm(ckv[:n_p], mla_w_kv_up, layer=j, out_dtype=BF16, tn=1024, name="mla_kv_up")
            o_p = _flash(q, kv, krb, batch=batch, seq_len=seq_len, scale=attn_scale)
            q_lat, q_rp = _q_absorb(q[n_p:], mla_w_kv_up, j, t=t_s)
            o_lat = _decode(q_lat.reshape(n_seq, n_h * t_s, KV_LORA),
                            q_rp.reshape(n_seq, n_h * t_s, HEAD),
                            ckv[n_p:].reshape(n_seq, t_s, KV_LORA),
                            kr[n_p:].reshape(n_seq, t_s, LANES),
                            cache_mla_latent, cache_kr_t, page_table, j, scale=attn_scale)
            o_s = _o_up(o_lat.reshape(n_seq, n_h, t_s, KV_LORA), mla_w_kv_up, j)
            y = _mm(jnp.concatenate([o_p, o_s], axis=0), mla_w_out, layer=j, out_dtype=F32,
                    tn=1024, name="mla_out")
            lat_p.append(ckv[:n_p].reshape(batch, seq_len, KV_LORA))
            kr_p.append(kr[:n_p, :QK_ROPE].reshape(batch, seq_len, QK_ROPE))
            lat_s.append(ckv[n_p:].reshape(n_seq, t_s, KV_LORA))
            kr_s.append(kr[n_p:, :QK_ROPE].reshape(n_seq, t_s, QK_ROPE))
        x, h2, top_idx, gates = norm(
            x, y, mods_p, mods_s, l, gate_k=2, ln_g=ln_g[l, 0], ln_b=ln_b[l, 0], mod_k=(3, 4),
            h_dtype=F32, tok_major=True, router=(moe_w_router[l], moe_b_router[l]), alpha=alpha,
            name="norm_mix")
        y = _moe(h2, top_idx, gates, moe_w_gu, moe_b_gu, moe_w_down, moe_b_down, l)
        if l + 1 < depth:
            x, h = norm(x, y, mods_p, mods_s, l, gate_k=5, ln_g=ln_g[l, 1], ln_b=ln_b[l, 1],
                        mod_k=(0, 1), h_layer=l + 1, alpha=alpha, name="norm_moe")
        else:
            x = norm(x, y, mods_p, mods_s, l, gate_k=5, ln_g=ln_g[l, 1], ln_b=ln_b[l, 1],
                     alpha=alpha, name="norm_last")[0]
    return (x[:n_p].reshape(batch, seq_len, d), x[n_p:].reshape(n_seq, t_s, d),
            jnp.stack(lat_p), jnp.stack(kr_p), jnp.stack(hg_p),
            jnp.stack(lat_s), jnp.stack(kr_s), jnp.stack(hg_s))
```

```python
import functools
import math

import jax
import jax.numpy as jnp
from jax import lax
from jax.experimental import pallas as pl
from jax.experimental.pallas import tpu as pltpu

F32 = jnp.float32
BF16 = jnp.bfloat16
I32 = jnp.int32

LANES = 128
VMEM_LIMIT_BYTES = 56 * 1024 * 1024

HEAD = 128
QK_ROPE = 64
Q_LORA = 512
KV_LORA = 512
TOP_K = 4
SWIGLU_LIMIT = 7.0
SWIGLU_ALPHA = 1.702
NORM_EPS = 1e-5
FORGET_FLOOR = 1e-30
ROPE_THETA = 10000.0
MASK_VALUE = -1e30
HIGHEST = lax.Precision.HIGHEST

LN_TM = 256
MM_TM = 512
HG_TB = 128
HG_SUB = 16
HG_G = 4
HG_GS = 2
HG_SB = 8
FL_TQ = 512
DEC_P = 16
MOE_TM = 256
MOE_RB = 6
MOE_TF = 512
MOE_SPARE = 4
GATHER_T = 256
COMBINE_T = 128


def _cparams(sem):
    return pltpu.CompilerParams(dimension_semantics=sem, vmem_limit_bytes=VMEM_LIMIT_BYTES)


def _fit_tile(n, t, align=LANES):
    t = min(t, n) // align * align
    while n % t:
        t -= align
    return t


def _dot(a, b, dims, precision=None):
    return lax.dot_general(a, b, (dims, ((), ())), precision=precision,
                           preferred_element_type=F32)


def _dot_nn(a, b, precision=None):
    return _dot(a, b, ((1,), (0,)), precision)


def _dot_nt(a, b, precision=None):
    return _dot(a, b, ((1,), (1,)), precision)


def _dot_tn(a, b, precision=None):
    return _dot(a, b, ((0,), (0,)), precision)


def _sigmoid(x):
    return 1.0 / (1.0 + jnp.exp(-x))


def _rope128(x, cos_t, sin_a, sin_b):
    return (x * cos_t + pltpu.roll(x, 96, axis=1) * sin_a + pltpu.roll(x, 32, axis=1) * sin_b)


def _mm_kernel(*refs, has_bias, silu_in, has_rope, m_axis):
    x_ref, w_ref = refs[0], refs[1]
    b_ref = refs[2] if has_bias else None
    o_ref, wbf = refs[-2], refs[-1]

    @pl.when(pl.program_id(m_axis) == 0)
    def _():
        wbf[...] = w_ref[...].astype(BF16)

    x = x_ref[...]
    if silu_in:
        x = x.astype(F32)
        x = x * _sigmoid(x)
    acc = _dot_nn(x.astype(BF16), wbf[...])
    if has_bias:
        acc = acc + b_ref[...]
    if has_rope:
        cos_t, sin_a, sin_b = refs[-5][...], refs[-4][...], refs[-3][...]
        pieces = []
        for c in range(0, acc.shape[1], 2 * HEAD):
            pieces.append(acc[:, c:c + HEAD])
            pieces.append(_rope128(acc[:, c + HEAD:c + 2 * HEAD], cos_t, sin_a, sin_b))
        acc = jnp.concatenate(pieces, axis=1)
    o_ref[...] = acc.astype(o_ref.dtype)


def _mm(x, w, bias=None, *, layer=None, out_dtype=F32, tm=MM_TM, tn=512, silu_in=False,
        rope=None, name="mm"):
    M, K = x.shape
    N = w.shape[-1]
    tm = _fit_tile(M, tm, 8)
    tn = _fit_tile(N, tn)
    assert M % tm == 0 and N % tn == 0 and w.shape[-2] == K
    batched = w.ndim == 3 and layer is None
    has_bias = bias is not None
    if w.ndim == 2:
        w = w[None]
        layer = 0
    if has_bias:
        bias = bias.reshape(w.shape[0], 1, N)
    if batched:
        grid = (w.shape[0], N // tn, M // tm)
        x_spec = pl.BlockSpec((tm, K), lambda l, n, m: (m, 0))
        w_spec = pl.BlockSpec((None, K, tn), lambda l, n, m: (l, 0, n))
        b_spec = pl.BlockSpec((None, 1, tn), lambda l, n, m: (l, 0, n))
        o_spec = pl.BlockSpec((None, tm, tn), lambda l, n, m: (l, m, n))
        out_shape = jax.ShapeDtypeStruct((w.shape[0], M, N), out_dtype)
        sem = ("arbitrary", "arbitrary", "arbitrary")
        m_axis = 2
    else:
        grid = (N // tn, M // tm)
        x_spec = pl.BlockSpec((tm, K), lambda n, m: (m, 0))
        w_spec = pl.BlockSpec((None, K, tn), lambda n, m: (layer, 0, n))
        b_spec = pl.BlockSpec((None, 1, tn), lambda n, m: (layer, 0, n))
        o_spec = pl.BlockSpec((tm, tn), lambda n, m: (m, n))
        out_shape = jax.ShapeDtypeStruct((M, N), out_dtype)
        sem = ("arbitrary", "arbitrary")
        m_axis = 1
    in_specs = [x_spec, w_spec] + ([b_spec] if has_bias else [])
    args = (x, w) + ((bias,) if has_bias else ())
    if rope is not None:
        assert not batched and tn % (2 * HEAD) == 0
        in_specs += [pl.BlockSpec((tm, LANES), lambda n, m: (m, 0))] * 3
        args += tuple(rope)
    return pl.pallas_call(
        functools.partial(_mm_kernel, has_bias=has_bias, silu_in=silu_in,
                          has_rope=rope is not None, m_axis=m_axis),
        grid=grid, in_specs=in_specs, out_specs=o_spec, out_shape=out_shape,
        scratch_shapes=[pltpu.VMEM((K, tn), BF16)],
        compiler_params=_cparams(sem), name=name)(*args)


def _topk_softmax(logits, k):
    tm, n_e = logits.shape
    lane = lax.broadcasted_iota(I32, logits.shape, 1).astype(F32)
    l = logits
    vals, idxs = [], []
    for _ in range(k):
        m = jnp.max(l, axis=-1, keepdims=True)
        idx = jnp.min(jnp.where(l == m, lane, float(n_e)), axis=-1, keepdims=True)
        vals.append(m)
        idxs.append(idx)
        l = jnp.where(lane == idx, -jnp.inf, l)
    es = [jnp.exp(v - vals[0]) for v in vals]
    tot = es[0]
    for e in es[1:]:
        tot = tot + e
    k_lane = lax.broadcasted_iota(I32, (tm, k), 1)
    idx_out = jnp.zeros((tm, k), F32)
    gate_out = jnp.zeros((tm, k), F32)
    for j in range(k):
        idx_out = jnp.where(k_lane == j, idxs[j], idx_out)
        gate_out = jnp.where(k_lane == j, es[j] / tot, gate_out)
    return idx_out.astype(I32), gate_out


def _store_tok_major(ref, val):
    t, d = val.shape
    rows = d // LANES
    for k in range(rows):
        ref[pl.ds(k, t, stride=rows), :] = val[:, k * LANES:(k + 1) * LANES].astype(ref.dtype)


def _load_tok_major(ref, t, d):
    rows = d // LANES
    return jnp.concatenate([ref[pl.ds(k, t, stride=rows), :] for k in range(rows)], axis=1)


def _norm_kernel(*refs, n_p_tiles, alpha, first, has_h, has_router, top_k, tok_major):
    it = iter(refs)
    x_ref = next(it)
    is_p = pl.program_id(0) < n_p_tiles
    x = x_ref[...]
    if first:
        xn = x
    else:
        y_ref, gp, gs, g_ref, b_ref = next(it), next(it), next(it), next(it), next(it)
        gate = jnp.where(is_p, gp[...], gs[...])
        z = alpha * x + (1.0 + gate) * y_ref[...].astype(F32)
        mu = jnp.mean(z, axis=-1, keepdims=True)
        zc = z - mu
        var = jnp.mean(zc * zc, axis=-1, keepdims=True)
        xn = zc * lax.rsqrt(var + NORM_EPS) * g_ref[...] + b_ref[...]
    if has_h:
        sp, ss, hp, hs = next(it), next(it), next(it), next(it)
        scale = jnp.where(is_p, sp[...], ss[...])
        shift = jnp.where(is_p, hp[...], hs[...])
        h = xn * (1.0 + scale) + shift
    if has_router:
        wr, br = next(it), next(it)
    if not first:
        xn_ref = next(it)
        xn_ref[...] = xn
    if has_h:
        h_ref = next(it)
        if tok_major:
            _store_tok_major(h_ref, h)
        else:
            h_ref[...] = h.astype(h_ref.dtype)
    if has_router:
        idx_ref, gate_ref = next(it), next(it)
        logits = _dot_nn(h, wr[...], HIGHEST) + br[...]
        idx, gates = _topk_softmax(logits, top_k)
        idx_ref[...] = idx
        gate_ref[...] = gates


def _norm_mod(x, y, mods_p, mods_s, layer, *, seq_len, n_prompt, gate_k=None, ln_g=None,
              ln_b=None, mod_k=None, h_layer=None, h_dtype=BF16, tok_major=False, router=None,
              alpha=1.0, name="norm"):
    N, D = x.shape
    tm = LN_TM
    assert N % tm == 0 and n_prompt % tm == 0 and seq_len % tm == 0
    n_p_tiles = n_prompt // tm
    n_seq = mods_p.shape[1]
    first = y is None
    has_h = mod_k is not None
    has_router = router is not None
    h_layer = layer if h_layer is None else h_layer

    def p_spec(lyr, k):
        return pl.BlockSpec((None, None, 1, D),
                            lambda i: (lyr, jnp.minimum(i * tm // seq_len, n_seq - 1), 0, k))

    def s_spec(lyr, k):
        return pl.BlockSpec((None, tm, D), lambda i: (lyr, jnp.maximum(i - n_p_tiles, 0), k))

    row_spec = pl.BlockSpec((tm, D), lambda i: (i, 0))
    vec_spec = pl.BlockSpec((1, D), lambda i: (0, 0))
    args, in_specs = [x], [row_spec]
    if not first:
        args += [y, mods_p, mods_s, ln_g.reshape(1, D), ln_b.reshape(1, D)]
        in_specs += [row_spec, p_spec(layer, gate_k), s_spec(layer, gate_k), vec_spec, vec_spec]
    if has_h:
        shift_k, scale_k = mod_k
        args += [mods_p, mods_s, mods_p, mods_s]
        in_specs += [p_spec(h_layer, scale_k), s_spec(h_layer, scale_k),
                     p_spec(h_layer, shift_k), s_spec(h_layer, shift_k)]
    out_shape, out_specs = [], []
    if has_router:
        w_r, b_r = router
        n_e = w_r.shape[-1]
        args += [w_r, b_r.reshape(1, n_e)]
        in_specs += [pl.BlockSpec((D, n_e), lambda i: (0, 0)),
                     pl.BlockSpec((1, n_e), lambda i: (0, 0))]
    if not first:
        out_shape.append(jax.ShapeDtypeStruct((N, D), F32))
        out_specs.append(row_spec)
    if has_h and tok_major:
        rows = D // LANES
        out_shape.append(jax.ShapeDtypeStruct((N * rows, LANES), h_dtype))
        out_specs.append(pl.BlockSpec((tm * rows, LANES), lambda i: (i, 0)))
    elif has_h:
        out_shape.append(jax.ShapeDtypeStruct((N, D), h_dtype))
        out_specs.append(row_spec)
    if has_router:
        out_shape += [jax.ShapeDtypeStruct((N, TOP_K), I32), jax.ShapeDtypeStruct((N, TOP_K), F32)]
        out_specs += [pl.BlockSpec((tm, TOP_K), lambda i: (i, 0))] * 2
    return pl.pallas_call(
        functools.partial(_norm_kernel, n_p_tiles=n_p_tiles, alpha=alpha, first=first,
                          has_h=has_h, has_router=has_router, top_k=TOP_K, tok_major=tok_major),
        grid=(N // tm,), in_specs=in_specs, out_specs=out_specs, out_shape=out_shape,
        compiler_params=_cparams(("arbitrary",)), name=name)(*args)


def _hgrn_units(units, sub, pads=None):
    n_u = len(units)
    tb = units[0][0].shape[0]
    ns = tb // sub
    row = lax.broadcasted_iota(I32, (tb, tb), 0)
    col = lax.broadcasted_iota(I32, (tb, tb), 1)
    tri = jnp.where(col <= row, 1.0, 0.0).astype(F32)
    prev = jnp.where(col < (row // sub) * sub, 1.0, 0.0).astype(F32)
    ones = jnp.ones((tb, HEAD), F32)
    pos = lax.broadcasted_iota(I32, (tb, 1), 0) % sub

    q, lf, kin, v = [], [], [], []
    for qr, fr, ir, gr, lb, ng, s_prev in units:
        q.append(qr * _sigmoid(qr))
        forget = lb + (1.0 - lb) * _sigmoid(fr)
        lf.append(jnp.log(jnp.maximum(forget, FORGET_FLOOR)))
        kin.append((1.0 - lb) * _sigmoid(-fr))
        v.append(ir)
    cum = [_dot_nn(tri, x, HIGHEST) for x in lf]
    if ns > 1:
        ref = [_dot_nn(prev, x, HIGHEST) for x in lf]
        loc = [c - r for c, r in zip(cum, ref)]
    else:
        ref, loc = None, cum
    tot_col = [_dot_tn(x, ones, HIGHEST) for x in lf]

    o = [_dot_nn((q[u] * jnp.exp(cum[u])).astype(BF16), units[u][6].astype(BF16))
         for u in range(n_u)]

    s_new = []
    for u in range(n_u):
        k_e = (kin[u] * jnp.exp(cum[u][tb - 1:tb] - cum[u])).astype(BF16)
        s_new.append(jnp.exp(tot_col[u]) * units[u][6] + _dot_tn(k_e, v[u].astype(BF16)))

    if ns > 1:
        for u in range(n_u):
            parts = [o[u][:sub]]
            for i in range(1, ns):
                lo = i * sub
                q_t = (q[u][lo:lo + sub] * jnp.exp(loc[u][lo:lo + sub])).astype(BF16)
                k_t = (kin[u][:lo]
                       * jnp.exp(jnp.minimum(ref[u][lo:lo + 1] - cum[u][:lo], 0.0))).astype(BF16)
                att = _dot_nt(q_t, k_t)
                parts.append(o[u][lo:lo + sub]
                             + _dot_nn(att.astype(BF16), v[u][:lo].astype(BF16)))
            o[u] = jnp.concatenate(parts, axis=0)

    out = []
    for u in range(n_u):
        if pads is not None:
            for pad, val in zip(pads[u], (kin[u], loc[u], v[u])):
                pad[pl.ds(sub, tb), :] = val
        acc = o[u]
        for d in range(sub):
            if d == 0:
                k_sh, b_sh, v_sh = kin[u], loc[u], v[u]
            elif pads is not None:
                k_sh, b_sh, v_sh = (pad[pl.ds(sub - d, tb), :] for pad in pads[u])
            else:
                k_sh, b_sh, v_sh = (pltpu.roll(val, d, axis=0)
                                    for val in (kin[u], loc[u], v[u]))
            p = q[u] * k_sh * jnp.exp(jnp.minimum(loc[u] - b_sh, 0.0))
            a = jnp.sum(p, axis=-1, keepdims=True)
            acc = acc + jnp.where(pos >= d, a, 0.0) * v_sh
        gr, ng = units[u][3], units[u][5]
        acc = acc * lax.rsqrt(jnp.mean(acc * acc, axis=-1, keepdims=True) + NORM_EPS) * ng
        out.append((acc * _sigmoid(gr), s_new[u]))
    return out


def _hgrn_prompt_kernel(q_ref, f_ref, i_ref, g_ref, lb_ref, ng_ref, o_ref, s_out, state,
                        *pads, n_heads, sub):
    c = pl.program_id(2)

    @pl.when(c == 0)
    def _():
        state[...] = jnp.zeros_like(state)
        for pad in pads:
            pad[...] = jnp.zeros_like(pad)

    cols = [slice(h * HEAD, (h + 1) * HEAD) for h in range(n_heads)]
    units = [(q_ref[:, sl], f_ref[:, sl], i_ref[:, sl], g_ref[:, sl], lb_ref[:, sl],
              ng_ref[...], state[h]) for h, sl in enumerate(cols)]
    res = _hgrn_units(units, sub, [pads[3 * h:3 * h + 3] for h in range(n_heads)])
    for h, (o, s_new) in enumerate(res):
        o_ref[:, cols[h]] = o.astype(o_ref.dtype)
        state[h] = s_new

    @pl.when(c == pl.num_programs(2) - 1)
    def _():
        s_out[...] = state[...]


def _hgrn_prompt(proj, lb, ng, *, batch, seq_len, row0=0):
    hf = proj.shape[1] // 4
    n_h = hf // HEAD
    g, tb, sub = min(HG_G, n_h), HG_TB, HG_SUB
    assert n_h % g == 0 and seq_len % tb == 0 and row0 % tb == 0
    nc = seq_len // tb
    ncol = hf // (g * HEAD)
    r0 = row0 // tb

    def seg_spec(seg):
        return pl.BlockSpec((tb, g * HEAD), lambda b, hg, c: (r0 + b * nc + c, seg * ncol + hg))

    return pl.pallas_call(
        functools.partial(_hgrn_prompt_kernel, n_heads=g, sub=sub),
        grid=(batch, n_h // g, nc),
        in_specs=[seg_spec(0), seg_spec(1), seg_spec(2), seg_spec(3),
                  pl.BlockSpec((1, g * HEAD), lambda b, hg, c: (0, hg)),
                  pl.BlockSpec((1, HEAD), lambda b, hg, c: (0, 0))],
        out_specs=[pl.BlockSpec((tb, g * HEAD), lambda b, hg, c: (b * nc + c, hg)),
                   pl.BlockSpec((None, g, HEAD, HEAD), lambda b, hg, c: (b, hg, 0, 0))],
        out_shape=[jax.ShapeDtypeStruct((batch * seq_len, hf), BF16),
                   jax.ShapeDtypeStruct((batch, n_h, HEAD, HEAD), F32)],
        scratch_shapes=[pltpu.VMEM((g, HEAD, HEAD), F32)]
        + [pltpu.VMEM((sub + tb, HEAD), F32)] * (3 * g),
        compiler_params=_cparams(("arbitrary", "arbitrary", "arbitrary")),
        name="hgrn_prompt")(proj, proj, proj, proj, lb.reshape(1, hf), ng.reshape(1, HEAD))


def _hgrn_sample_kernel(q_ref, f_ref, i_ref, g_ref, lb_ref, ng_ref, s_ref, o_ref, s_out, *,
                        n_seq, n_heads, t):
    where = [(s, h, slice(s * t, (s + 1) * t), slice(h * HEAD, (h + 1) * HEAD))
             for s in range(n_seq) for h in range(n_heads)]
    units = [(q_ref[rows, sl], f_ref[rows, sl], i_ref[rows, sl], g_ref[rows, sl],
              lb_ref[:, sl], ng_ref[...], s_ref[s, h]) for s, h, rows, sl in where]
    for (s, h, rows, sl), (o, s_new) in zip(where, _hgrn_units(units, t)):
        o_ref[rows, sl] = o.astype(o_ref.dtype)
        s_out[s, h] = s_new


def _hgrn_sample(proj, lb, ng, s0, layer, *, t, row0):
    hf = proj.shape[1] // 4
    n_h = hf // HEAD
    n_seq = s0.shape[1]
    sb, g = HG_SB, HG_GS
    rb = sb * t
    assert n_seq % sb == 0 and n_h % g == 0 and row0 % rb == 0
    ncol = hf // (g * HEAD)
    r0 = row0 // rb

    def seg_spec(seg):
        return pl.BlockSpec((rb, g * HEAD), lambda s, hg: (r0 + s, seg * ncol + hg))

    return pl.pallas_call(
        functools.partial(_hgrn_sample_kernel, n_seq=sb, n_heads=g, t=t),
        grid=(n_seq // sb, n_h // g),
        in_specs=[seg_spec(0), seg_spec(1), seg_spec(2), seg_spec(3),
                  pl.BlockSpec((1, g * HEAD), lambda s, hg: (0, hg)),
                  pl.BlockSpec((1, HEAD), lambda s, hg: (0, 0)),
                  pl.BlockSpec((None, sb, g, HEAD, HEAD), lambda s, hg: (layer, s, hg, 0, 0))],
        out_specs=[pl.BlockSpec((rb, g * HEAD), lambda s, hg: (s, hg)),
                   pl.BlockSpec((sb, g, HEAD, HEAD), lambda s, hg: (s, hg, 0, 0))],
        out_shape=[jax.ShapeDtypeStruct((n_seq * t, hf), BF16),
                   jax.ShapeDtypeStruct(s0.shape[1:], F32)],
        compiler_params=_cparams(("arbitrary", "arbitrary")),
        name="hgrn_sample")(proj, proj, proj, proj, lb.reshape(1, hf), ng.reshape(1, HEAD), s0)


def _mla_post_kernel(a_ref, qg_ref, kg_ref, cos_ref, sa_ref, sb_ref,
                     cq_ref, ckv_ref, kr_ref, krb_ref):
    a = a_ref[...]
    cq = a[:, :Q_LORA]
    cq = cq * lax.rsqrt(jnp.mean(cq * cq, axis=-1, keepdims=True) + NORM_EPS) * qg_ref[...]
    cq_ref[...] = cq.astype(cq_ref.dtype)
    ckv = a[:, Q_LORA:Q_LORA + KV_LORA]
    ckv = ckv * lax.rsqrt(jnp.mean(ckv * ckv, axis=-1, keepdims=True) + NORM_EPS) * kg_ref[...]
    ckv_ref[...] = ckv
    kr = _rope128(a[:, Q_LORA + KV_LORA:], cos_ref[...], sa_ref[...], sb_ref[...])
    kr_ref[...] = kr
    krb_ref[...] = kr.astype(krb_ref.dtype)


def _mla_post(a, q_g, kv_g, cos_t, sin_a, sin_b):
    n = a.shape[0]
    tm = LN_TM
    row = lambda w: pl.BlockSpec((tm, w), lambda i: (i, 0))
    vec = lambda w: pl.BlockSpec((1, w), lambda i: (0, 0))
    return pl.pallas_call(
        _mla_post_kernel, grid=(n // tm,),
        in_specs=[row(a.shape[1]), vec(Q_LORA), vec(KV_LORA), row(LANES), row(LANES), row(LANES)],
        out_specs=[row(Q_LORA), row(KV_LORA), row(LANES), row(LANES)],
        out_shape=[jax.ShapeDtypeStruct((n, Q_LORA), BF16),
                   jax.ShapeDtypeStruct((n, KV_LORA), F32),
                   jax.ShapeDtypeStruct((n, LANES), F32),
                   jax.ShapeDtypeStruct((n, LANES), BF16)],
        compiler_params=_cparams(("arbitrary",)), name="mla_post")(
            a, q_g.reshape(1, Q_LORA), kv_g.reshape(1, KV_LORA), cos_t, sin_a, sin_b)


def _flash_kernel(q_ref, kv_ref, kr_ref, o_ref, *, scale, t):
    qi = pl.program_id(2)
    q = (q_ref[...].astype(F32) * scale).astype(BF16)

    def block(kc, carry, diagonal):
        m_prev, l_prev, acc = carry
        rows = pl.ds(pl.multiple_of(kc * t, t), t)
        k = jnp.concatenate([kv_ref[rows, :HEAD], kr_ref[rows, :]], axis=1)
        s = _dot_nt(q, k)
        if diagonal:
            qpos = lax.broadcasted_iota(I32, (t, t), 0)
            kpos = lax.broadcasted_iota(I32, (t, t), 1)
            s = jnp.where(kpos <= qpos, s, MASK_VALUE)
        m_new = jnp.maximum(m_prev, jnp.max(s, axis=-1, keepdims=True))
        alpha = jnp.exp(m_prev - m_new)
        p = jnp.exp(s - m_new)
        l_new = alpha * l_prev + jnp.sum(p, axis=-1, keepdims=True)
        acc = alpha * acc + _dot_nn(p.astype(BF16), kv_ref[rows, HEAD:])
        return m_new, l_new, acc

    init = (jnp.full((t, 1), MASK_VALUE, F32), jnp.zeros((t, 1), F32), jnp.zeros((t, HEAD), F32))
    carry = lax.fori_loop(
        0, qi // 2, lambda p, c: block(2 * p + 1, block(2 * p, c, False), False), init)
    carry = lax.cond(qi % 2 == 1, lambda c: block(qi - 1, c, False), lambda c: c, carry)
    _, l_fin, acc = block(qi, carry, True)
    o_ref[...] = (acc / l_fin).astype(o_ref.dtype)


def _flash(q, kv, krb, *, batch, seq_len, scale):
    n_h = q.shape[1] // (2 * HEAD)
    t = min(FL_TQ, seq_len)
    nq = seq_len // t
    return pl.pallas_call(
        functools.partial(_flash_kernel, scale=scale, t=t),
        grid=(batch, n_h, nq),
        in_specs=[pl.BlockSpec((t, 2 * HEAD), lambda b, h, qi: (b * nq + qi, h)),
                  pl.BlockSpec((seq_len, 2 * HEAD), lambda b, h, qi: (b, h)),
                  pl.BlockSpec((seq_len, LANES), lambda b, h, qi: (b, 0))],
        out_specs=pl.BlockSpec((t, HEAD), lambda b, h, qi: (b * nq + qi, h)),
        out_shape=jax.ShapeDtypeStruct((batch * seq_len, n_h * HEAD), BF16),
        compiler_params=_cparams(("arbitrary",) * 3), name="mla_flash")(q, kv, krb)


def _q_absorb_kernel(q_ref, w_ref, ql_ref, qr_ref, *, t):
    x = q_ref[...]
    n = x.shape[0]
    ql = _dot_nt(x[:, :HEAD], w_ref[...].astype(BF16))
    ql_ref[...] = ql.reshape(n // t, t, KV_LORA)
    qr_ref[...] = x[:, HEAD:].astype(F32).reshape(n // t, t, HEAD)


def _q_absorb(q_s, w_kv_up, layer, *, t):
    ns, w = q_s.shape
    n_h = w // (2 * HEAD)
    n_seq = ns // t
    return pl.pallas_call(
        functools.partial(_q_absorb_kernel, t=t), grid=(n_h,),
        in_specs=[pl.BlockSpec((ns, 2 * HEAD), lambda h: (0, h)),
                  pl.BlockSpec((None, KV_LORA, HEAD), lambda h: (layer, 0, 2 * h))],
        out_specs=[pl.BlockSpec((n_seq, None, t, KV_LORA), lambda h: (0, h, 0, 0)),
                   pl.BlockSpec((n_seq, None, t, HEAD), lambda h: (0, h, 0, 0))],
        out_shape=[jax.ShapeDtypeStruct((n_seq, n_h, t, KV_LORA), F32),
                   jax.ShapeDtypeStruct((n_seq, n_h, t, HEAD), F32)],
        compiler_params=_cparams(("arbitrary",)), name="q_absorb")(q_s, w_kv_up)


def _decode_kernel(pt_ref, ql_ref, qr_ref, cn_ref, kn_ref, lat_hbm, kr_hbm, o_ref,
                   lat_buf, kr_buf, sem, m_sc, l_sc, acc_sc, *, layer, scale, n_pg, t):
    j = pl.program_id(1)
    nj = pl.num_programs(1)
    step = pl.program_id(0) * nj + j
    n_steps = pl.num_programs(0) * nj
    slot = step % 2

    def page_copies(stp, slt):
        out = []
        for i in range(n_pg):
            pg = pt_ref[stp * n_pg + i]
            out.append(pltpu.make_async_copy(lat_hbm.at[layer, pg], lat_buf.at[slt, i],
                                             sem.at[0, slt]))
            out.append(pltpu.make_async_copy(kr_hbm.at[layer, pg], kr_buf.at[slt, i],
                                             sem.at[1, slt]))
        return out

    @pl.when(step == 0)
    def _():
        for c in page_copies(0, 0):
            c.start()

    @pl.when(step + 1 < n_steps)
    def _():
        for c in page_copies(step + 1, 1 - slot):
            c.start()

    for c in page_copies(step, slot):
        c.wait()

    @pl.when(j == 0)
    def _():
        m_sc[...] = jnp.full_like(m_sc, MASK_VALUE)
        l_sc[...] = jnp.zeros_like(l_sc)
        acc_sc[...] = jnp.zeros_like(acc_sc)

    q = ql_ref[...].astype(BF16)
    qr = qr_ref[...][:, :QK_ROPE].astype(BF16)

    def update(s, values):
        m_prev = m_sc[...]
        m_new = jnp.maximum(m_prev, jnp.max(s, axis=-1, keepdims=True))
        alpha = jnp.exp(m_prev - m_new)
        p = jnp.exp(s - m_new)
        l_sc[...] = alpha * l_sc[...] + jnp.sum(p, axis=-1, keepdims=True)
        acc = alpha * acc_sc[...]
        w = 0
        for val in values:
            n = val.shape[0]
            acc = acc + _dot_nn(p[:, w:w + n].astype(BF16), val)
            w += n
        acc_sc[...] = acc
        m_sc[...] = m_new

    lats = [lat_buf[slot, i].astype(BF16) for i in range(n_pg)]
    s = jnp.concatenate(
        [_dot_nt(q, lats[i]) + _dot_nn(qr, kr_buf[slot, i].astype(BF16)) for i in range(n_pg)],
        axis=1) * scale
    update(s, lats)

    @pl.when(j == nj - 1)
    def _():
        cn = cn_ref[...].astype(BF16)
        kn = kn_ref[...][:, :QK_ROPE].astype(BF16)
        s_new = (_dot_nt(q, cn) + _dot_nt(qr, kn)) * scale
        rows = s_new.shape[0]
        qpos = lax.broadcasted_iota(I32, (rows, t), 0) % t
        kpos = lax.broadcasted_iota(I32, (rows, t), 1)
        update(jnp.where(kpos <= qpos, s_new, MASK_VALUE), [cn])
        o_ref[...] = acc_sc[...] / l_sc[...]


def _decode(q_lat, q_rp, ckv_new, kr_new, cache_lat, cache_kr, page_table, layer, *, scale):
    n_seq, rows, _ = q_lat.shape
    t = ckv_new.shape[1]
    n_pages = page_table.shape[1]
    page = cache_lat.shape[2]
    n_pg = min(DEC_P, n_pages)
    assert n_pages % n_pg == 0

    seq3 = lambda r, w: pl.BlockSpec((None, r, w), lambda b, j, pt: (b, 0, 0))
    hbm = pl.BlockSpec(memory_space=pl.ANY)
    grid_spec = pltpu.PrefetchScalarGridSpec(
        num_scalar_prefetch=1, grid=(n_seq, n_pages // n_pg),
        in_specs=[seq3(rows, KV_LORA), seq3(rows, HEAD), seq3(t, KV_LORA), seq3(t, HEAD),
                  hbm, hbm],
        out_specs=seq3(rows, KV_LORA),
        scratch_shapes=[pltpu.VMEM((2, n_pg, page, KV_LORA), F32),
                        pltpu.VMEM((2, n_pg, QK_ROPE, page), F32),
                        pltpu.SemaphoreType.DMA((2, 2)),
                        pltpu.VMEM((rows, 1), F32), pltpu.VMEM((rows, 1), F32),
                        pltpu.VMEM((rows, KV_LORA), F32)])
    return pl.pallas_call(
        functools.partial(_decode_kernel, layer=layer, scale=scale, n_pg=n_pg, t=t),
        grid_spec=grid_spec,
        out_shape=jax.ShapeDtypeStruct((n_seq, rows, KV_LORA), F32),
        compiler_params=_cparams(("arbitrary", "arbitrary")), name="mla_decode")(
            page_table.reshape(-1), q_lat, q_rp, ckv_new, kr_new, cache_lat, cache_kr)


def _o_up_kernel(x_ref, w_ref, o_ref):
    x = x_ref[...]
    n_seq, t, c = x.shape
    o = _dot_nn(x.reshape(n_seq * t, c).astype(BF16), w_ref[...].astype(BF16))
    o_ref[...] = o.astype(o_ref.dtype)


def _o_up(o_lat, w_kv_up, layer):
    n_seq, n_h, t, c = o_lat.shape
    return pl.pallas_call(
        _o_up_kernel, grid=(n_h,),
        in_specs=[pl.BlockSpec((n_seq, None, t, c), lambda h: (0, h, 0, 0)),
                  pl.BlockSpec((None, c, HEAD), lambda h: (layer, 0, 2 * h + 1))],
        out_specs=pl.BlockSpec((n_seq * t, HEAD), lambda h: (0, h)),
        out_shape=jax.ShapeDtypeStruct((n_seq * t, n_h * HEAD), BF16),
        compiler_params=_cparams(("arbitrary",)), name="o_up")(o_lat, w_kv_up)


def _gather_kernel(idx_ref, nxt_ref, src_ref, o_ref, buf, sem, *, n_tok, rows, d):
    i = pl.program_id(0)
    slot = i % 2

    def tok_copy(ref, r, slt):
        src = src_ref.at[pl.ds(pl.multiple_of(ref[0, r], rows), rows)]
        dst = buf.at[slt, pl.ds(pl.multiple_of(r * rows, rows), rows)]
        return pltpu.make_async_copy(src, dst, sem.at[slt])

    def start_all(ref, slt):
        def body(r2, c):
            tok_copy(ref, 2 * r2, slt).start(priority=0)
            tok_copy(ref, 2 * r2 + 1, slt).start(priority=1)
            return c
        lax.fori_loop(0, n_tok // 2, body, 0, unroll=4)

    @pl.when(i == 0)
    def _():
        start_all(idx_ref, 0)

    @pl.when(i + 1 < pl.num_programs(0))
    def _():
        start_all(nxt_ref, 1 - slot)

    def wait(r, c):
        tok_copy(idx_ref, r, slot).wait()
        return c

    lax.fori_loop(0, n_tok, wait, 0, unroll=8)
    o_ref[...] = _load_tok_major(buf.at[slot], n_tok, d).astype(o_ref.dtype)


def _gather_rows(src, idx, d, *, out_dtype=BF16):
    n = idx.shape[0]
    rows = d // LANES
    t = GATHER_T
    assert n % t == 0 and t % 2 == 0
    steps = n // t
    idx3 = idx.reshape(steps, 1, t)
    return pl.pallas_call(
        functools.partial(_gather_kernel, n_tok=t, rows=rows, d=d), grid=(steps,),
        in_specs=[pl.BlockSpec((None, 1, t), lambda i: (i, 0, 0), memory_space=pltpu.SMEM),
                  pl.BlockSpec((None, 1, t), lambda i: (jnp.minimum(i + 1, steps - 1), 0, 0),
                               memory_space=pltpu.SMEM),
                  pl.BlockSpec(memory_space=pl.ANY)],
        out_specs=pl.BlockSpec((t, d), lambda i: (i, 0)),
        out_shape=jax.ShapeDtypeStruct((n, d), out_dtype),
        scratch_shapes=[pltpu.VMEM((2, t * rows, LANES), F32), pltpu.SemaphoreType.DMA((2,))],
        compiler_params=_cparams(("arbitrary",)), name="moe_gather")(idx3, idx3, src)


def _combine_kernel(idx_ref, nxt_ref, gate_ref, src_ref, o_ref, buf, g_sc, acc_sc, sem, *,
                    n_tok, k, rows, d):
    i = pl.program_id(0)
    slot = i % 2

    def tok_copy(ref, tok, j, slt):
        src = src_ref.at[pl.ds(pl.multiple_of(ref[0, tok * k + j], rows), rows)]
        dst = buf.at[slt, j, pl.ds(pl.multiple_of(tok * rows, rows), rows)]
        return pltpu.make_async_copy(src, dst, sem.at[slt])

    def start_all(ref, slt):
        def body(tok, c):
            for j in range(k):
                tok_copy(ref, tok, j, slt).start(priority=j % 2)
            return c
        lax.fori_loop(0, n_tok, body, 0, unroll=2)

    @pl.when(i == 0)
    def _():
        start_all(idx_ref, 0)

    @pl.when(i + 1 < pl.num_programs(0))
    def _():
        start_all(nxt_ref, 1 - slot)

    g = gate_ref[...]
    for r in range(rows):
        g_sc[pl.ds(r, n_tok, stride=rows), :] = g

    def wait(tok, c):
        for j in range(k):
            tok_copy(idx_ref, tok, j, slot).wait()
        return c

    lax.fori_loop(0, n_tok, wait, 0, unroll=2)
    gr = g_sc[...]
    acc = gr[:, 0:1] * buf[slot, 0]
    for j in range(1, k):
        acc = acc + gr[:, j:j + 1] * buf[slot, j]
    acc_sc[...] = acc
    o_ref[...] = _load_tok_major(acc_sc, n_tok, d)


def _combine(yb, dest, gates, d):
    n, k = dest.shape
    rows = d // LANES
    t = COMBINE_T
    assert n % t == 0
    steps = n // t
    dest3 = dest.reshape(steps, 1, t * k)
    return pl.pallas_call(
        functools.partial(_combine_kernel, n_tok=t, k=k, rows=rows, d=d), grid=(steps,),
        in_specs=[pl.BlockSpec((None, 1, t * k), lambda i: (i, 0, 0), memory_space=pltpu.SMEM),
                  pl.BlockSpec((None, 1, t * k),
                               lambda i: (jnp.minimum(i + 1, steps - 1), 0, 0),
                               memory_space=pltpu.SMEM),
                  pl.BlockSpec((t, k), lambda i: (i, 0)),
                  pl.BlockSpec(memory_space=pl.ANY)],
        out_specs=pl.BlockSpec((t, d), lambda i: (i, 0)),
        out_shape=jax.ShapeDtypeStruct((n, d), F32),
        scratch_shapes=[pltpu.VMEM((2, k, t * rows, LANES), F32), pltpu.VMEM((t * rows, k), F32),
                        pltpu.VMEM((t * rows, LANES), F32), pltpu.SemaphoreType.DMA((2,))],
        compiler_params=_cparams(("arbitrary",)), name="moe_combine")(dest3, dest3, gates, yb)


def _moe_ffn_kernel(se_ref, sb_ref, sn_ref, sz_ref, x_ref, wg_ref, wu_ref, wd_ref, bg_ref,
                    bu_ref, bd_ref, *rest, has_prior):
    o_ref, wg_bf, wu_bf, wd_bf, acc = rest[1:] if has_prior else rest
    s, j, r = pl.program_id(0), pl.program_id(1), pl.program_id(2)
    nj = pl.num_programs(1)
    n_sub = sn_ref[s]

    @pl.when(jnp.logical_and(j == nj - 1, r < sz_ref[s]))
    def _():
        o_ref[...] = jnp.zeros_like(o_ref)

    @pl.when(jnp.logical_and(r == 0, n_sub > 0))
    def _():
        wg_bf[...] = wg_ref[...].astype(BF16)
        wu_bf[...] = wu_ref[...].astype(BF16)
        wd_bf[...] = wd_ref[...].astype(BF16)

    @pl.when(r < n_sub)
    def _():
        x = x_ref[...]
        gate = jnp.minimum(_dot_nn(x, wg_bf[...]) + bg_ref[...], SWIGLU_LIMIT)
        up = jnp.clip(_dot_nn(x, wu_bf[...]) + bu_ref[...], -SWIGLU_LIMIT, SWIGLU_LIMIT)
        act = (up + 1.0) * (gate * _sigmoid(SWIGLU_ALPHA * gate))
        part = _dot_nn(act.astype(BF16), wd_bf[...])

        @pl.when(j == 0)
        def _():
            acc[r] = part

        @pl.when(jnp.logical_and(j > 0, j < nj - 1))
        def _():
            acc[r] = acc[r] + part

        @pl.when(j == nj - 1)
        def _():
            _store_tok_major(o_ref, acc[r] + part + bd_ref[...])


def _moe_ffn(xb, w_gu, b_gu, w_down, b_down, layer, sb_expert, sb_block, sb_nsub, sb_nzero,
             prior=None):
    d = xb.shape[1]
    tm, rb, tf = MOE_TM, MOE_RB, MOE_TF
    rows = d // LANES
    n_blocks = xb.shape[0] // tm
    d_ff = w_down.shape[2]
    n_e = w_down.shape[1]
    nj = d_ff // tf
    n_super = sb_expert.shape[0]
    assert nj >= 2 and rb >= 4

    def jeff(s, j, sn):
        return jnp.where(sn[s] > 0, j, nj - 1)

    def x_map(s, j, r, se, sb, sn, sz):
        return (sb[s] + jnp.minimum(r, jnp.maximum(sn[s] - 1, 0)), 0)

    def o_map(s, j, r, se, sb, sn, sz):
        cnt = jnp.where(sn[s] > 0, sn[s], sz[s])
        r_last = jnp.minimum(r, jnp.maximum(cnt - 1, 0))
        return (sb[s] + jnp.where(j == nj - 1, r_last, 0), 0)

    def w_spec(shape, fn, switch_r=None):
        def index_map(s, j, r, se, sb, sn, sz):
            if switch_r is None:
                return fn(se[s], jeff(s, j, sn))
            wrap = j == nj - 1
            s2 = jnp.where(wrap, jnp.minimum(s + 1, n_super - 1), s)
            j2 = jnp.where(wrap, jnp.where(s == n_super - 1, nj - 1, 0), j + 1)
            ahead = r >= switch_r
            s_use = jnp.where(ahead, s2, s)
            return fn(se[s_use], jeff(s_use, jnp.where(ahead, j2, j), sn))
        return pl.BlockSpec(shape, index_map)

    grid_spec = pltpu.PrefetchScalarGridSpec(
        num_scalar_prefetch=4, grid=(n_super, nj, rb),
        in_specs=[
            pl.BlockSpec((tm, d), x_map),
            w_spec((None, None, d, tf), lambda e, jj: (layer, e, 0, jj), 1),
            w_spec((None, None, d, tf), lambda e, jj: (layer, e, 0, nj + jj), 2),
            w_spec((None, None, tf, d), lambda e, jj: (layer, e, jj, 0), 3),
            w_spec((None, None, 1, tf), lambda e, jj: (layer, e, 0, jj)),
            w_spec((None, None, 1, tf), lambda e, jj: (layer, e, 0, nj + jj)),
            w_spec((None, None, 1, d), lambda e, jj: (layer, e, 0, 0)),
        ] + ([pl.BlockSpec(memory_space=pl.ANY)] if prior is not None else []),
        out_specs=pl.BlockSpec((tm * rows, LANES), o_map),
        scratch_shapes=[pltpu.VMEM((d, tf), BF16), pltpu.VMEM((d, tf), BF16),
                        pltpu.VMEM((tf, d), BF16), pltpu.VMEM((rb, tm, d), F32)])
    n_l = w_gu.shape[0]
    args = (sb_expert, sb_block, sb_nsub, sb_nzero, xb, w_gu, w_gu, w_down,
            b_gu.reshape(n_l, n_e, 1, 2 * d_ff), b_gu.reshape(n_l, n_e, 1, 2 * d_ff),
            b_down.reshape(n_l, n_e, 1, d))
    if prior is not None:
        args += (prior,)
    return pl.pallas_call(
        functools.partial(_moe_ffn_kernel, has_prior=prior is not None), grid_spec=grid_spec,
        out_shape=jax.ShapeDtypeStruct((n_blocks * tm * rows, LANES), F32),
        input_output_aliases={len(args) - 1: 0} if prior is not None else {},
        compiler_params=_cparams(("arbitrary", "arbitrary", "arbitrary")), name="moe_ffn")(*args)


def _route(top_idx, n_e):
    n, k = top_idx.shape
    a = n * k
    tm, rb = MOE_TM, MOE_RB
    flat_e = top_idx.reshape(a)
    onehot = (flat_e[:, None] == jnp.arange(n_e, dtype=I32)[None, :]).astype(I32)
    csum = jnp.cumsum(onehot, axis=0)
    rank = jnp.sum((csum - 1) * onehot, axis=1)
    counts = csum[-1]
    nblk = (counts + tm - 1) // tm
    blk_end = jnp.cumsum(nblk)
    blk_start = blk_end - nblk
    dest = blk_start[flat_e] * tm + rank
    n_blocks = a // tm + n_e
    row_tok = jnp.zeros((n_blocks * tm,), I32).at[dest].set(jnp.arange(a, dtype=I32) // k)
    n_super = -(-n_blocks // rb) + n_e
    nsup = (nblk + rb - 1) // rb
    sup_end = jnp.cumsum(nsup)
    sup_start = sup_end - nsup
    sid = jnp.arange(n_super, dtype=I32)
    e_of = jnp.minimum(jnp.searchsorted(sup_end, sid, side="right"), n_e - 1).astype(I32)
    used = sid < sup_end[-1]
    local = sid - sup_start[e_of]
    sb_block = blk_start[e_of] + local * rb
    sb_nsub = jnp.clip(nblk[e_of] - local * rb, 0, rb)
    last = jnp.maximum(sup_end[-1] - 1, 0)
    sb_expert = jnp.where(used, e_of, e_of[last]).astype(I32)
    z_start = blk_end[-1] + (sid - sup_end[-1]) * rb
    sb_nzero = jnp.where(used, 0, jnp.clip(n_blocks - z_start, 0, rb)).astype(I32)
    sb_block = jnp.where(used, sb_block, jnp.minimum(z_start, n_blocks - 1)).astype(I32)
    sb_nsub = jnp.where(used, sb_nsub, 0).astype(I32)
    return row_tok, dest.reshape(n, k).astype(I32), sb_expert, sb_block, sb_nsub, sb_nzero


def _moe(h2, top_idx, gates, w_gu, b_gu, w_down, b_down, layer):
    n_e, _, d = w_down.shape[1:]
    rows = d // LANES
    row_tok, dest, sb_e, sb_b, sb_n, sb_z = _route(top_idx, n_e)
    xb = _gather_rows(h2, row_tok * rows, d)
    tables = (sb_e, sb_b, sb_n, sb_z)
    n_main = min(sb_e.shape[0], n_e + MOE_SPARE)
    ffn = functools.partial(_moe_ffn, xb, w_gu, b_gu, w_down, b_down, layer)
    yb = ffn(*(t[:n_main] for t in tables))
    if n_main < sb_e.shape[0]:
        rest = tuple(t[n_main:] for t in tables)
        needed = jnp.any(rest[2] > 0) | jnp.any(rest[3] > 0)
        yb = lax.cond(needed, lambda y: ffn(*rest, prior=y), lambda y: y, yb)
    return _combine(yb, dest * rows, gates, d)


def _rope_tables(pos):
    half = QK_ROPE // 2
    inv = ROPE_THETA ** (-jnp.arange(half, dtype=F32) / half)
    ang = pos.astype(F32)[:, None] * inv[None, :]
    cos, sin = jnp.cos(ang), jnp.sin(ang)
    z = jnp.zeros_like(cos)
    cos_t = jnp.concatenate([cos, cos, z, z], axis=1)
    sin_a = jnp.concatenate([-sin, z, z, z], axis=1)
    sin_b = jnp.concatenate([z, sin, z, z], axis=1)
    return cos_t, sin_a, sin_b


def kernel(x_prompt, x_sample, c_prompt, c_sample, cache_mla_latent, cache_mla_krope, state_hgrn, page_table, ada_w, ada_b, ln_g, ln_b, hg_lb_logits, hg_w_in, hg_norm_g, hg_w_out, mla_w_in, mla_q_norm_g, mla_kv_norm_g, mla_w_q_up, mla_w_kv_up, mla_w_out, moe_w_router, moe_b_router, moe_w_gu, moe_b_gu, moe_w_down, moe_b_down):
    batch, seq_len, d = x_prompt.shape
    n_seq, t_s, _ = x_sample.shape
    depth = ada_w.shape[0]
    n_p, n_s = batch * seq_len, n_seq * t_s
    alpha = (2 * depth) ** 0.25
    past = page_table.shape[1] * cache_mla_latent.shape[2]
    n_h = mla_w_out.shape[1] // HEAD
    qk_dim = HEAD + QK_ROPE
    attn_scale = qk_dim ** -0.5

    x = jnp.concatenate([x_prompt.reshape(n_p, d), x_sample.reshape(n_s, d)], axis=0)

    n_c = batch + n_seq
    c_rows = -(-n_c // 8) * 8
    c_all = jnp.concatenate([c_prompt, c_sample, jnp.zeros((c_rows - n_c, d), F32)], axis=0)
    mods = _mm(c_all, ada_w, ada_b, out_dtype=F32, tm=c_rows, tn=1024, silu_in=True, name="adaln")
    mods_p = mods[:, :batch].reshape(depth, batch, 1, 6 * d)
    mods_s = jnp.repeat(mods[:, batch:n_c], t_s, axis=1)

    lb_p = jax.nn.softmax(hg_lb_logits.astype(F32), axis=0)
    lower_bounds = jnp.cumsum(lb_p, axis=0) - lb_p[0]

    pos = jnp.concatenate([jnp.tile(jnp.arange(seq_len), batch), jnp.tile(past + jnp.arange(t_s), n_seq)])
    cos_t, sin_a, sin_b = _rope_tables(pos)

    norm = functools.partial(_norm_mod, seq_len=seq_len, n_prompt=n_p)
    h = norm(x, None, mods_p, mods_s, 0, mod_k=(0, 1), name="modulate0")[0]
    cache_kr_t = jnp.swapaxes(cache_mla_krope, 2, 3)

    lat_p, kr_p, hg_p, lat_s, kr_s, hg_s = [], [], [], [], [], []
    for l in range(depth):
        j = l // 2
        if l % 2 == 0:
            proj = _mm(h, hg_w_in, layer=j, out_dtype=F32, tn=1024, name="hgrn_in")
            o_p, sp = _hgrn_prompt(proj, lower_bounds[j], hg_norm_g[j], batch=batch, seq_len=seq_len)
            o_s, ss = _hgrn_sample(proj, lower_bounds[j], hg_norm_g[j], state_hgrn, j, t=t_s, row0=n_p)
            hg_p.append(sp)
            hg_s.append(ss)
            y = _mm(jnp.concatenate([o_p, o_s], axis=0), hg_w_out, layer=j, out_dtype=F32,
                    tn=1024, name="hgrn_out")
        else:
            w_in = jnp.pad(mla_w_in[j], ((0, 0), (0, LANES - QK_ROPE)))
            a = _mm(h, w_in, out_dtype=F32, tn=w_in.shape[1], name="mla_in")
            cq, ckv, kr, krb = _mla_post(a, mla_q_norm_g[j], mla_kv_norm_g[j], cos_t, sin_a, sin_b)
            w_q = jnp.pad(mla_w_q_up[j].reshape(Q_LORA, n_h, qk_dim),
                          ((0, 0), (0, 0), (0, 2 * HEAD - qk_dim))).reshape(Q_LORA, n_h * 2 * HEAD)
            q = _mm(cq, w_q, out_dtype=BF16, tn=1024, rope=(cos_t, sin_a, sin_b), name="mla_q_up")
            kv = _mm(ckv[:n_p], mla_w_kv_up, layer=j, out_dtype=BF16, tn=1024, name="mla_kv_up")
            o_p = _flash(q, kv, krb, batch=batch, seq_len=seq_len, scale=attn_scale)
            q_lat, q_rp = _q_absorb(q[n_p:], mla_w_kv_up, j, t=t_s)
            o_lat = _decode(q_lat.reshape(n_seq, n_h * t_s, KV_LORA),
                            q_rp.reshape(n_seq, n_h * t_s, HEAD),
                            ckv[n_p:].reshape(n_seq, t_s, KV_LORA),
                            kr[n_p:].reshape(n_seq, t_s, LANES),
                            cache_mla_latent, cache_kr_t, page_table, j, scale=attn_scale)
            o_s = _o_up(o_lat.reshape(n_seq, n_h, t_s, KV_LORA), mla_w_kv_up, j)
            y = _mm(jnp.concatenate([o_p, o_s], axis=0), mla_w_out, layer=j, out_dtype=F32,
                    tn=1024, name="mla_out")
            lat_p.append(ckv[:n_p].reshape(batch, seq_len, KV_LORA))
            kr_p.append(kr[:n_p, :QK_ROPE].reshape(batch, seq_len, QK_ROPE))
            lat_s.append(ckv[n_p:].reshape(n_seq, t_s, KV_LORA))
            kr_s.append(kr[n_p:, :QK_ROPE].reshape(n_seq, t_s, QK_ROPE))
        x, h2, top_idx, gates = norm(
            x, y, mods_p, mods_s, l, gate_k=2, ln_g=ln_g[l, 0], ln_b=ln_b[l, 0], mod_k=(3, 4),
            h_dtype=F32, tok_major=True, router=(moe_w_router[l], moe_b_router[l]), alpha=alpha,
            name="norm_mix")
        y = _moe(h2, top_idx, gates, moe_w_gu, moe_b_gu, moe_w_down, moe_b_down, l)
        if l + 1 < depth:
            x, h = norm(x, y, mods_p, mods_s, l, gate_k=5, ln_g=ln_g[l, 1], ln_b=ln_b[l, 1],
                        mod_k=(0, 1), h_layer=l + 1, alpha=alpha, name="norm_moe")
        else:
            x = norm(x, y, mods_p, mods_s, l, gate_k=5, ln_g=ln_g[l, 1], ln_b=ln_b[l, 1],
                     alpha=alpha, name="norm_last")[0]
    return (x[:n_p].reshape(batch, seq_len, d), x[n_p:].reshape(n_seq, t_s, d),
            jnp.stack(lat_p), jnp.stack(kr_p), jnp.stack(hg_p),
            jnp.stack(lat_s), jnp.stack(kr_s), jnp.stack(hg_s))
```
